```python
import jax, jax.numpy as jnp
from jax import lax
import numpy as np

D_MODEL = 1024
BATCH = 8
SEQ = 2048
DEPTH = 2
DEC_BATCH = 32
DEC_SEQ = 1
PAST_LEN = 16384
PAGE_SIZE = 128

N_HEADS_A = 8
HEAD_DIM_A = D_MODEL // 16
WIDTH_A = N_HEADS_A * HEAD_DIM_A
Q_BLOCK = 128
FORGET_BIAS_LO = 3.0
FORGET_BIAS_HI = 12.0
FORGET_W_SCALE = 0.1
WIDTH_B = D_MODEL // 4
CONV_WIDTH = 3
POOL_WINDOWS = (2, 4, 8, 16)
N_GROUPS_C = len(POOL_WINDOWS)
WIDTH_C = D_MODEL // 4
GROUP_DIM_C = WIDTH_C // N_GROUPS_C
POOL_HIST = max(POOL_WINDOWS) - 1
N_BRANCHES = 3
OFF_Q = 0
OFF_K = OFF_Q + WIDTH_A
OFF_V = OFF_K + WIDTH_A
OFF_F = OFF_V + WIDTH_A
OFF_BB = OFF_F + N_HEADS_A
OFF_CB = OFF_BB + WIDTH_B
OFF_HB = OFF_CB + WIDTH_B
OFF_PC = OFF_HB + WIDTH_B
OFF_G = OFF_PC + WIDTH_C
N_IN = OFF_G + N_BRANCHES * D_MODEL
N_EXPERTS = 64
TOP_K = 6
N_EXPERT_GROUPS = 8
TOPK_GROUPS = 4
D_EXPERT = D_MODEL // 4
D_SHARED = D_EXPERT
ROUTED_SCALE = 2.5
N_MOD = 6
RMS_EPS = 1e-6

kernel_name = "fox_conv_pool_gated_moe_decoder_step"


def rmsnorm(x, g):
    xf = x.astype(jnp.float32)
    y = xf * lax.rsqrt(jnp.mean(xf * xf, axis=-1, keepdims=True) + RMS_EPS)
    return (y * g.astype(jnp.float32)).astype(x.dtype)


def fox_prompt(q, k, v, lf):
    B, S = q.shape[0], q.shape[1]
    c = jnp.cumsum(lf, axis=1).transpose(0, 2, 1)
    k_pos = jnp.arange(S)
    scale = HEAD_DIM_A ** -0.5

    def block(i):
        q0 = i * Q_BLOCK
        qb = lax.dynamic_slice_in_dim(q, q0, Q_BLOCK, axis=1)
        cq = lax.dynamic_slice_in_dim(c, q0, Q_BLOCK, axis=2)
        s = jnp.einsum('bqhd,bkhd->bhqk', qb, k, preferred_element_type=jnp.float32) * scale
        s = s + cq[..., :, None] - c[..., None, :]
        q_pos = q0 + jnp.arange(Q_BLOCK)
        s = jnp.where(q_pos[:, None] >= k_pos[None, :], s, -jnp.inf)
        p = jax.nn.softmax(s, axis=-1).astype(v.dtype)
        return jnp.einsum('bhqk,bkhd->bqhd', p, v)

    o = lax.map(block, jnp.arange(S // Q_BLOCK))
    return o.transpose(1, 0, 2, 3, 4).reshape(B, S, WIDTH_A)


def fox_sample(q, k, v, lf, k_past, v_past, lf_past):
    B, L = q.shape[0], q.shape[1]
    P = k_past.shape[1]
    scale = HEAD_DIM_A ** -0.5
    cq = jnp.cumsum(lf, axis=1).transpose(0, 2, 1)
    lfp = lf_past.astype(jnp.float32)
    d_past = (lax.cumsum(lfp, axis=1, reverse=True) - lfp).transpose(0, 2, 1)
    s_past = jnp.einsum('bqhd,bkhd->bhqk', q, k_past, preferred_element_type=jnp.float32) * scale
    s_past = s_past + cq[..., :, None] + d_past[..., None, :]
    s_new = jnp.einsum('bqhd,bkhd->bhqk', q, k, preferred_element_type=jnp.float32) * scale
    s_new = s_new + cq[..., :, None] - cq[..., None, :]
    causal = jnp.arange(L)[:, None] >= jnp.arange(L)[None, :]
    s_new = jnp.where(causal, s_new, -jnp.inf)
    p = jax.nn.softmax(jnp.concatenate([s_past, s_new], axis=-1), axis=-1).astype(v.dtype)
    o = (jnp.einsum('bhqk,bkhd->bqhd', p[..., :P], v_past)
         + jnp.einsum('bhqk,bkhd->bqhd', p[..., P:], v))
    return o.reshape(B, L, WIDTH_A)


def short_conv(u, hist, w):
    L = u.shape[1]
    ue = jnp.concatenate([hist.astype(u.dtype), u], axis=1)
    y = ue[:, 0:L] * w[0]
    for j in range(1, CONV_WIDTH):
        y = y + ue[:, j:j + L] * w[j]
    return y, ue[:, -(CONV_WIDTH - 1):]


def multi_pool(p, hist, pos0, pool_w, pool_scale):
    B, L = p.shape[0], p.shape[1]
    pe = jnp.concatenate([hist.astype(p.dtype), p], axis=1)
    pef = pe.astype(jnp.float32)
    cs = jnp.concatenate([jnp.zeros((B, 1, WIDTH_C), jnp.float32), jnp.cumsum(pef, axis=1)], axis=1)
    end = cs[:, POOL_HIST + 1:]
    pos = pos0 + jnp.arange(L)
    means = []
    for g, w in enumerate(POOL_WINDOWS):
        lo, hi = g * GROUP_DIM_C, (g + 1) * GROUP_DIM_C
        start = cs[:, POOL_HIST + 1 - w:POOL_HIST + 1 - w + L, lo:hi]
        cnt = jnp.minimum(w, pos + 1).astype(jnp.float32)[None, :, None]
        means.append((end[..., lo:hi] - start) / cnt)
    d = (jnp.concatenate(means, axis=-1) - pef[:, POOL_HIST:]).astype(p.dtype)
    d = d.reshape(B, L, N_GROUPS_C, GROUP_DIM_C)
    y = jnp.einsum('blgc,gce->blge', d, pool_w).reshape(B, L, WIDTH_C) * pool_scale
    return y, pe[:, -POOL_HIST:]


def token_mixers(h, lp, attn_fn, conv_hist, pool_hist, pos0):
    B, L = h.shape[0], h.shape[1]
    z = h @ lp['w_in']
    q = z[..., OFF_Q:OFF_K].reshape(B, L, N_HEADS_A, HEAD_DIM_A)
    k = z[..., OFF_K:OFF_V].reshape(B, L, N_HEADS_A, HEAD_DIM_A)
    v = z[..., OFF_V:OFF_F].reshape(B, L, N_HEADS_A, HEAD_DIM_A)
    lf = jax.nn.log_sigmoid((z[..., OFF_F:OFF_BB] + lp['b_forget']).astype(jnp.float32))
    a_out = attn_fn(q, k, v, lf)
    b_gate = z[..., OFF_BB:OFF_CB]
    c_gate = z[..., OFF_CB:OFF_HB]
    h_b = z[..., OFF_HB:OFF_PC]
    y_conv, conv_new = short_conv(c_gate * h_b, conv_hist, lp['conv_w'])
    b_out = b_gate * y_conv
    c_out, pool_new = multi_pool(z[..., OFF_PC:OFF_G], pool_hist, pos0, lp['pool_w'], lp['pool_scale'])
    gates = jax.nn.sigmoid(z[..., OFF_G:].reshape(B, L, N_BRANCHES, D_MODEL))
    merged = (gates[..., 0, :] * (a_out @ lp['w_br_attn'])
              + gates[..., 1, :] * (b_out @ lp['w_br_conv'])
              + gates[..., 2, :] * (c_out @ lp['w_br_pool']))
    return merged @ lp['w_out'], (k, v, lf), conv_new, pool_new


def moe_ffn(h, lp):
    logits = jnp.einsum('bld,de->ble', h, lp['w_router'], preferred_element_type=jnp.float32)
    s = jax.nn.sigmoid(logits)
    sel = s + lp['router_bias'].astype(jnp.float32)
    sg = sel.reshape(sel.shape[:-1] + (N_EXPERT_GROUPS, N_EXPERTS // N_EXPERT_GROUPS))
    gscore = jnp.sum(lax.top_k(sg, 2)[0], axis=-1)
    _, gidx = lax.top_k(gscore, TOPK_GROUPS)
    gmask = jnp.sum(jax.nn.one_hot(gidx, N_EXPERT_GROUPS, dtype=jnp.float32), axis=-2) > 0
    masked = jnp.where(gmask[..., None], sg, -jnp.inf).reshape(sel.shape)
    _, eidx = lax.top_k(masked, TOP_K)
    w = jnp.take_along_axis(s, eidx, axis=-1)
    w = w / jnp.sum(w, axis=-1, keepdims=True) * ROUTED_SCALE
    combine = jnp.einsum('blk,blke->ble', w, jax.nn.one_hot(eidx, N_EXPERTS, dtype=jnp.float32)).astype(h.dtype)
    wg, wu, wd = lp['we_gate'], lp['we_up'], lp['we_down']

    def per_row(args):
        xr, cr = args
        g = jnp.einsum('ld,edf->lef', xr, wg)
        u = jnp.einsum('ld,edf->lef', xr, wu)
        a = jax.nn.silu(g) * u * cr[..., None]
        return jnp.einsum('lef,efd->ld', a, wd)

    routed = lax.map(per_row, (h, combine))
    shared = (jax.nn.silu(h @ lp['ws_gate']) * (h @ lp['ws_up'])) @ lp['ws_down']
    return routed + shared


def layer(x, c, lp, attn_fn, conv_hist, pool_hist, pos0):
    mod = (jax.nn.silu(c) @ lp['w_ada'] + lp['b_ada']).reshape(c.shape[0], 1, N_MOD, D_MODEL)
    shift1, scale1, gate1 = mod[:, :, 0], mod[:, :, 1], mod[:, :, 2]
    shift2, scale2, gate2 = mod[:, :, 3], mod[:, :, 4], mod[:, :, 5]
    h = rmsnorm(x, lp['g_pre_mix']) * (1 + scale1) + shift1
    m, kvl, conv_new, pool_new = token_mixers(h, lp, attn_fn, conv_hist, pool_hist, pos0)
    x = x + gate1 * rmsnorm(m, lp['g_post_mix'])
    h = rmsnorm(x, lp['g_pre_ffn']) * (1 + scale2) + shift2
    x = x + gate2 * rmsnorm(moe_ffn(h, lp), lp['g_post_ffn'])
    return x, kvl, conv_new, pool_new


def setup_inputs(seed: int = 0) -> dict:
    key = jax.random.key(seed)
    ks = jax.random.split(key, 33)

    def nrm(i, shape, s=1.0):
        return jax.random.normal(ks[i], shape, jnp.float32) * s

    d = D_MODEL
    n_pages = PAST_LEN // PAGE_SIZE
    n_used = DEC_BATCH * n_pages
    n_phys = n_used + max(1, n_used // 4)
    page_table = jax.random.permutation(ks[7], n_phys)[:n_used].reshape(DEC_BATCH, n_pages).astype(jnp.int32)
    head_bias = jnp.linspace(FORGET_BIAS_LO, FORGET_BIAS_HI, N_HEADS_A, dtype=jnp.float32)
    w_in = nrm(10, (DEPTH, d, N_IN), d ** -0.5)
    w_in = w_in.at[:, :, OFF_F:OFF_BB].multiply(FORGET_W_SCALE)
    return {
        "x_prompt": nrm(0, (BATCH, SEQ, d)),
        "x_sample": nrm(1, (DEC_BATCH, DEC_SEQ, d)),
        "cache_k": nrm(2, (DEPTH, n_phys, PAGE_SIZE, N_HEADS_A, HEAD_DIM_A)),
        "cache_v": nrm(3, (DEPTH, n_phys, PAGE_SIZE, N_HEADS_A, HEAD_DIM_A)),
        "cache_logf": jax.nn.log_sigmoid(head_bias + nrm(4, (DEPTH, n_phys, PAGE_SIZE, N_HEADS_A), 0.1)),
        "state_conv": nrm(5, (DEPTH, DEC_BATCH, CONV_WIDTH - 1, WIDTH_B)),
        "state_pool": nrm(6, (DEPTH, DEC_BATCH, POOL_HIST, WIDTH_C)),
        "page_table": page_table,
        "c_prompt": nrm(8, (BATCH, d)),
        "c_sample": nrm(9, (DEC_BATCH, d)),
        "w_in": w_in,
        "b_forget": head_bias + nrm(11, (DEPTH, N_HEADS_A), 0.1),
        "conv_w": nrm(12, (DEPTH, CONV_WIDTH, WIDTH_B), CONV_WIDTH ** -0.5),
        "pool_w": nrm(13, (DEPTH, N_GROUPS_C, GROUP_DIM_C, GROUP_DIM_C), GROUP_DIM_C ** -0.5),
        "pool_scale": 1.0 + nrm(14, (DEPTH, WIDTH_C), 0.1),
        "w_br_attn": nrm(15, (DEPTH, WIDTH_A, d), WIDTH_A ** -0.5),
        "w_br_conv": nrm(16, (DEPTH, WIDTH_B, d), WIDTH_B ** -0.5),
        "w_br_pool": nrm(17, (DEPTH, WIDTH_C, d), WIDTH_C ** -0.5),
        "w_out": nrm(18, (DEPTH, d, d), d ** -0.5),
        "g_pre_mix": 1.0 + nrm(19, (DEPTH, d), 0.1),
        "g_post_mix": 1.0 + nrm(20, (DEPTH, d), 0.1),
        "g_pre_ffn": 1.0 + nrm(21, (DEPTH, d), 0.1),
        "g_post_ffn": 1.0 + nrm(22, (DEPTH, d), 0.1),
        "w_ada": nrm(23, (DEPTH, d, N_MOD * d), 0.5 * d ** -0.5),
        "b_ada": nrm(24, (DEPTH, N_MOD * d), 0.02),
        "w_router": nrm(25, (DEPTH, d, N_EXPERTS), d ** -0.5),
        "router_bias": nrm(26, (DEPTH, N_EXPERTS), 0.01),
        "we_gate": nrm(27, (DEPTH, N_EXPERTS, d, D_EXPERT), d ** -0.5),
        "we_up": nrm(28, (DEPTH, N_EXPERTS, d, D_EXPERT), d ** -0.5),
        "we_down": nrm(29, (DEPTH, N_EXPERTS, D_EXPERT, d), D_EXPERT ** -0.5),
        "ws_gate": nrm(30, (DEPTH, d, D_SHARED), d ** -0.5),
        "ws_up": nrm(31, (DEPTH, d, D_SHARED), d ** -0.5),
        "ws_down": nrm(32, (DEPTH, D_SHARED, d), D_SHARED ** -0.5),
    }


def reference(x_prompt, x_sample, cache_k, cache_v, cache_logf, state_conv, state_pool, page_table,
              c_prompt, c_sample, w_in, b_forget, conv_w, pool_w, pool_scale, w_br_attn, w_br_conv,
              w_br_pool, w_out, g_pre_mix, g_post_mix, g_pre_ffn, g_post_ffn, w_ada, b_ada, w_router,
              router_bias, we_gate, we_up, we_down, ws_gate, ws_up, ws_down):
    bp, bd = x_prompt.shape[0], x_sample.shape[0]
    past = page_table.shape[1] * cache_k.shape[2]
    xp, xs = x_prompt, x_sample
    kp_l, vp_l, lfp_l, cvp_l, plp_l = [], [], [], [], []
    ks_l, vs_l, lfs_l, cvs_l, pls_l = [], [], [], [], []
    for l in range(DEPTH):
        lp = dict(w_in=w_in[l], b_forget=b_forget[l], conv_w=conv_w[l], pool_w=pool_w[l],
                  pool_scale=pool_scale[l], w_br_attn=w_br_attn[l], w_br_conv=w_br_conv[l],
                  w_br_pool=w_br_pool[l], w_out=w_out[l], g_pre_mix=g_pre_mix[l],
                  g_post_mix=g_post_mix[l], g_pre_ffn=g_pre_ffn[l], g_post_ffn=g_post_ffn[l],
                  w_ada=w_ada[l], b_ada=b_ada[l], w_router=w_router[l], router_bias=router_bias[l],
                  we_gate=we_gate[l], we_up=we_up[l], we_down=we_down[l],
                  ws_gate=ws_gate[l], ws_up=ws_up[l], ws_down=ws_down[l])
        zc = jnp.zeros((bp, CONV_WIDTH - 1, WIDTH_B), xp.dtype)
        zp = jnp.zeros((bp, POOL_HIST, WIDTH_C), xp.dtype)
        xp, (k1, v1, lf1), cv1, pl1 = layer(xp, c_prompt, lp, fox_prompt, zc, zp, 0)
        k_past = cache_k[l][page_table].reshape(bd, past, N_HEADS_A, HEAD_DIM_A)
        v_past = cache_v[l][page_table].reshape(bd, past, N_HEADS_A, HEAD_DIM_A)
        lf_past = cache_logf[l][page_table].reshape(bd, past, N_HEADS_A)
        attn_s = lambda q, k, v, lf, kp=k_past, vp=v_past, lfp=lf_past: fox_sample(q, k, v, lf, kp, vp, lfp)
        xs, (k2, v2, lf2), cv2, pl2 = layer(xs, c_sample, lp, attn_s, state_conv[l], state_pool[l], past)
        kp_l.append(k1); vp_l.append(v1); lfp_l.append(lf1); cvp_l.append(cv1); plp_l.append(pl1)
        ks_l.append(k2); vs_l.append(v2); lfs_l.append(lf2); cvs_l.append(cv2); pls_l.append(pl2)
    k_prompt, v_prompt, logf_prompt = jnp.stack(kp_l), jnp.stack(vp_l), jnp.stack(lfp_l)
    conv_prompt, pool_prompt = jnp.stack(cvp_l), jnp.stack(plp_l)
    k_sample, v_sample, logf_sample = jnp.stack(ks_l), jnp.stack(vs_l), jnp.stack(lfs_l)
    conv_sample, pool_sample = jnp.stack(cvs_l), jnp.stack(pls_l)
    return (xp, xs, k_prompt, v_prompt, logf_prompt, conv_prompt, pool_prompt,
            k_sample, v_sample, logf_sample, conv_sample, pool_sample)
```

```python
import functools

import jax
import jax.numpy as jnp
from jax import lax
from jax.experimental import pallas as pl
from jax.experimental.pallas import tpu as pltpu

F32 = jnp.float32
BF16 = jnp.bfloat16

D_MODEL = 1024
DEPTH = 2
PAGE_SIZE = 128
N_HEADS = 8
HEAD_DIM = 64
WIDTH_A = N_HEADS * HEAD_DIM
WIDTH_B = 256
CONV_WIDTH = 3
POOL_WINDOWS = (2, 4, 8, 16)
WIDTH_C = 256
GROUP_DIM_C = WIDTH_C // len(POOL_WINDOWS)
POOL_HIST = max(POOL_WINDOWS) - 1
N_BRANCHES = 3
OFF_Q = 0
OFF_K = OFF_Q + WIDTH_A
OFF_V = OFF_K + WIDTH_A
OFF_F = OFF_V + WIDTH_A
OFF_BB = OFF_F + N_HEADS
OFF_CB = OFF_BB + WIDTH_B
OFF_HB = OFF_CB + WIDTH_B
OFF_PC = OFF_HB + WIDTH_B
OFF_G = OFF_PC + WIDTH_C
N_IN = OFF_G + N_BRANCHES * D_MODEL
N_EXPERTS = 64
TOP_K = 6
N_EXPERT_GROUPS = 8
GROUP_SIZE = N_EXPERTS // N_EXPERT_GROUPS
TOPK_GROUPS = 4
D_EXPERT = 256
ROUTED_SCALE = 2.5
N_MOD = 6
RMS_EPS = 1e-6

LANES = 128
SUBLANES = 8
F_PAD = LANES
NEG_BIG = -1e30
VMEM_LIMIT = 56 * 1024 * 1024

C_Q, C_K, C_V = 0, 512, 1024
C_BCH = 1536
C_PC = 2304
C_G = 2560
C_F = C_G + N_BRANCHES * D_MODEL
N_PACK = C_F + F_PAD

TM = 256
TE = 256
TQ = 512
TKV = 512
NP_STEP = 8


def _cparams(sem):
    return pltpu.CompilerParams(dimension_semantics=sem, vmem_limit_bytes=VMEM_LIMIT)


def _rms(x, g):
    ms = jnp.mean(x * x, axis=-1, keepdims=True)
    return x * lax.rsqrt(ms + RMS_EPS) * g


def _silu(x):
    return x * jax.nn.sigmoid(x)


def _log_sigmoid(x):
    return jnp.minimum(x, 0.0) - jnp.log1p(jnp.exp(-jnp.abs(x)))


def _bdot(a, b):
    return jnp.dot(a.astype(BF16), b.astype(BF16), preferred_element_type=F32)


def _mod_spec(tm, rows_per_mod):
    if rows_per_mod > 1:
        return pl.BlockSpec((1, N_MOD, D_MODEL), lambda i: (i // (rows_per_mod // tm), 0, 0))
    return pl.BlockSpec((tm, N_MOD, D_MODEL), lambda i: (i, 0, 0))


def _mod_kernel(c_ref, w_ref, b_ref, o_ref):
    o_ref[...] = _bdot(_silu(c_ref[...]), w_ref[...]) + b_ref[...]


def _modulation(c_all, w_ada, b_ada):
    n = c_all.shape[0]
    tn = 1536
    return pl.pallas_call(
        _mod_kernel,
        grid=(N_MOD * D_MODEL // tn,),
        in_specs=[pl.BlockSpec((n, D_MODEL), lambda j: (0, 0)),
                  pl.BlockSpec((D_MODEL, tn), lambda j: (0, j)),
                  pl.BlockSpec((1, tn), lambda j: (0, j))],
        out_specs=pl.BlockSpec((n, tn), lambda j: (0, j)),
        out_shape=jax.ShapeDtypeStruct((n, N_MOD * D_MODEL), F32),
        compiler_params=_cparams(("arbitrary",)),
        name="modulation",
    )(c_all, w_ada, b_ada.reshape(1, -1))


def _in_proj_kernel(x_ref, mod_ref, g_ref, w_ref, bf_ref,
                    q_ref, k_ref, v_ref, lf_ref, bb_ref, u_ref, pc_ref, gt_ref):
    h = _rms(x_ref[...], g_ref[...]) * (1.0 + mod_ref[:, 1, :]) + mod_ref[:, 0, :]
    hb = h.astype(BF16)

    def mm(lo, hi):
        return jnp.dot(hb, w_ref[:, lo:hi], preferred_element_type=F32)

    q_ref[...] = mm(C_Q, C_K)
    k_ref[...] = mm(C_K, C_V)
    v_ref[...] = mm(C_V, C_BCH)
    z = mm(C_BCH, C_PC)
    bb_ref[...] = z[:, :WIDTH_B]
    u_ref[...] = z[:, WIDTH_B:2 * WIDTH_B] * z[:, 2 * WIDTH_B:]
    pc_ref[...] = mm(C_PC, C_G)
    step = 512
    for j in range(N_BRANCHES * D_MODEL // step):
        gt_ref[:, j * step:(j + 1) * step] = jax.nn.sigmoid(mm(C_G + j * step, C_G + (j + 1) * step))
    lf_ref[...] = _log_sigmoid(mm(C_F, N_PACK) + bf_ref[...])


def _in_proj(x, mod, g, w_pack, bf_pad, tm, rows_per_mod):
    t = x.shape[0]
    widths = (WIDTH_A, WIDTH_A, WIDTH_A, F_PAD, WIDTH_B, WIDTH_B, WIDTH_C, N_BRANCHES * D_MODEL)
    return pl.pallas_call(
        _in_proj_kernel,
        grid=(t // tm,),
        in_specs=[pl.BlockSpec((tm, D_MODEL), lambda i: (i, 0)),
                  _mod_spec(tm, rows_per_mod),
                  pl.BlockSpec((1, D_MODEL), lambda i: (0, 0)),
                  pl.BlockSpec((D_MODEL, N_PACK), lambda i: (0, 0)),
                  pl.BlockSpec((1, F_PAD), lambda i: (0, 0))],
        out_specs=[pl.BlockSpec((tm, w), lambda i: (i, 0)) for w in widths],
        out_shape=[jax.ShapeDtypeStruct((t, w), F32) for w in widths],
        compiler_params=_cparams(("arbitrary",)),
        name="in_proj",
    )(x, mod, g, w_pack, bf_pad)


def _attn_kernel(q_ref, k_ref, v_ref, cq_ref, ck_ref, o_ref, m_sc, l_sc, acc_sc, *, tq, tk):
    qi = pl.program_id(2)
    kj = pl.program_id(3)
    nk = pl.num_programs(3)

    @pl.when(kj == 0)
    def _():
        m_sc[...] = jnp.full(m_sc.shape, NEG_BIG, F32)
        l_sc[...] = jnp.zeros(l_sc.shape, F32)
        acc_sc[...] = jnp.zeros(acc_sc.shape, F32)

    @pl.when(kj * tk <= qi * tq + (tq - 1))
    def _():
        lane = lax.broadcasted_iota(jnp.int32, (1, LANES), 1)
        first = lane < HEAD_DIM
        qs = q_ref[...] * (HEAD_DIM ** -0.5)
        kb = k_ref[...].astype(BF16)
        vb = v_ref[...].astype(BF16)
        rows = qi * tq + lax.broadcasted_iota(jnp.int32, (tq, 1), 0)
        cols = kj * tk + lax.broadcasted_iota(jnp.int32, (1, tk), 1)
        causal = rows >= cols
        for h in range(2):
            qh = jnp.where(first if h == 0 else jnp.logical_not(first), qs, 0.0).astype(BF16)
            s = lax.dot_general(qh, kb, (((1,), (1,)), ((), ())), preferred_element_type=F32)
            s = s + cq_ref[0, :, h:h + 1] - ck_ref[0, h:h + 1, :]
            s = jnp.where(causal, s, NEG_BIG)
            m_prev = m_sc[h]
            m_new = jnp.maximum(m_prev, jnp.max(s, axis=-1, keepdims=True))
            alpha = jnp.exp(m_prev - m_new)
            p = jnp.exp(s - m_new)
            l_sc[h] = alpha * l_sc[h] + jnp.sum(p, axis=-1, keepdims=True)
            acc_sc[h] = acc_sc[h] * alpha + jnp.dot(p.astype(BF16), vb, preferred_element_type=F32)
            m_sc[h] = m_new

    @pl.when(kj == nk - 1)
    def _():
        lane = lax.broadcasted_iota(jnp.int32, (1, LANES), 1)
        o_ref[...] = jnp.where(lane < HEAD_DIM, acc_sc[0] / l_sc[0], acc_sc[1] / l_sc[1])


def _prompt_attention(q, k, v, c_col, c_row, batch, seq):
    tq, tk = TQ, TKV
    nq, nk = seq // tq, seq // tk

    def last_kv(i):
        return (i * tq + tq - 1) // tk

    def kv_map(b, hp, i, j):
        return (b * nk + jnp.minimum(j, last_kv(i)), hp)

    return pl.pallas_call(
        functools.partial(_attn_kernel, tq=tq, tk=tk),
        grid=(batch, N_HEADS // 2, nq, nk),
        in_specs=[pl.BlockSpec((tq, LANES), lambda b, hp, i, j: (b * nq + i, hp)),
                  pl.BlockSpec((tk, LANES), kv_map),
                  pl.BlockSpec((tk, LANES), kv_map),
                  pl.BlockSpec((1, tq, 2), lambda b, hp, i, j: (hp, b * nq + i, 0)),
                  pl.BlockSpec((1, 2, tk),
                               lambda b, hp, i, j: (b * (N_HEADS // 2) + hp, 0,
                                                    jnp.minimum(j, last_kv(i))))],
        out_specs=pl.BlockSpec((tq, LANES), lambda b, hp, i, j: (b * nq + i, hp)),
        out_shape=jax.ShapeDtypeStruct((batch * seq, WIDTH_A), F32),
        scratch_shapes=[pltpu.VMEM((2, tq, 1), F32), pltpu.VMEM((2, tq, 1), F32),
                        pltpu.VMEM((2, tq, LANES), F32)],
        compiler_params=_cparams(("arbitrary",) * 4),
        name="prompt_attention",
    )(q, k, v, c_col, c_row)


PAGE_FLAT = PAGE_SIZE * N_HEADS
SCAN_SHIFTS = tuple(N_HEADS << s for s in range(7))


def _decode_kernel(pt_ref, q_ref, kn_ref, vn_ref, cq_ref, *rest, n_pages):
    k_refs = rest[:n_pages]
    v_refs = rest[n_pages:2 * n_pages]
    lf_refs = rest[2 * n_pages:3 * n_pages]
    o_ref = rest[3 * n_pages]
    m_sc, l_sc, acc_sc, carry_sc = rest[3 * n_pages + 1:]
    j = pl.program_id(1)
    nj = pl.num_programs(1)

    @pl.when(j == 0)
    def _():
        m_sc[...] = jnp.full(m_sc.shape, NEG_BIG, F32)
        l_sc[...] = jnp.zeros(l_sc.shape, F32)
        acc_sc[...] = jnp.zeros(acc_sc.shape, F32)
        carry_sc[...] = jnp.zeros(carry_sc.shape, F32)

    q2 = q_ref[0] * (HEAD_DIM ** -0.5)
    qb = q2.astype(BF16)
    lane = lax.broadcasted_iota(jnp.int32, (N_HEADS, PAGE_FLAT), 1)
    own = (lane % N_HEADS) == lax.broadcasted_iota(jnp.int32, (N_HEADS, PAGE_FLAT), 0)

    lf = jnp.concatenate([r[0, 0] for r in lf_refs], axis=0)
    plane = lax.broadcasted_iota(jnp.int32, lf.shape, 1)
    later = jnp.where(plane + N_HEADS < PAGE_FLAT, pltpu.roll(lf, PAGE_FLAT - N_HEADS, 1), 0.0)
    total = lf
    for s in SCAN_SHIFTS:
        later = later + jnp.where(plane + s < PAGE_FLAT, pltpu.roll(later, PAGE_FLAT - s, 1), 0.0)
        total = total + pltpu.roll(total, s, 1)

    carry = carry_sc[...] + cq_ref[0]
    scores = [None] * n_pages
    for i in reversed(range(n_pages)):
        bias = later[i:i + 1] + carry
        carry = carry + total[i:i + 1]
        kp = k_refs[i][0, 0].reshape(PAGE_FLAT, HEAD_DIM).astype(BF16)
        s = lax.dot_general(qb, kp, (((1,), (1,)), ((), ())), preferred_element_type=F32)
        scores[i] = jnp.where(own, s + bias, NEG_BIG)
    carry_sc[...] = carry - cq_ref[0]

    s_max = scores[0]
    for s in scores[1:]:
        s_max = jnp.maximum(s_max, s)
    m_prev = m_sc[...]
    m_new = jnp.maximum(m_prev, jnp.max(s_max, axis=-1, keepdims=True))
    alpha = jnp.exp(m_prev - m_new)
    acc = acc_sc[...] * alpha
    p_sum = jnp.zeros((N_HEADS, PAGE_FLAT), F32)
    for i in range(n_pages):
        p = jnp.exp(scores[i] - m_new)
        p_sum = p_sum + p
        vp = v_refs[i][0, 0].reshape(PAGE_FLAT, HEAD_DIM).astype(BF16)
        acc = acc + jnp.dot(p.astype(BF16), vp, preferred_element_type=F32)
    l_new = l_sc[...] * alpha + jnp.sum(p_sum, axis=-1, keepdims=True)
    m_sc[...] = m_new
    l_sc[...] = l_new
    acc_sc[...] = acc

    @pl.when(j == nj - 1)
    def _():
        s_new = jnp.sum(q2 * kn_ref[0], axis=-1, keepdims=True)
        m_f = jnp.maximum(m_new, s_new)
        a1 = jnp.exp(m_new - m_f)
        a2 = jnp.exp(s_new - m_f)
        o_ref[0] = (acc * a1 + a2 * vn_ref[0]) / (l_new * a1 + a2)


def _decode_attention(layer, q, k_new, v_new, cq_flat, cache_k, cache_v, cache_lf, page_table):
    nb, n_pages_total = page_table.shape
    npg = NP_STEP
    nj = n_pages_total // npg

    def page_map(i, ndim):
        def f(b, j, pt):
            return (layer, pt[b, (nj - 1 - j) * npg + i]) + (0,) * (ndim - 2)
        return f

    row = lambda b, j, pt: (b, 0, 0)
    head_blk = pl.BlockSpec((1, N_HEADS, HEAD_DIM), row)
    in_specs = [head_blk, head_blk, head_blk, pl.BlockSpec((1, 1, PAGE_FLAT), row)]
    in_specs += [pl.BlockSpec((1, 1, PAGE_SIZE, N_HEADS, HEAD_DIM), page_map(i, 5)) for i in range(npg)]
    in_specs += [pl.BlockSpec((1, 1, PAGE_SIZE, N_HEADS, HEAD_DIM), page_map(i, 5)) for i in range(npg)]
    in_specs += [pl.BlockSpec((1, 1, 1, PAGE_FLAT), page_map(i, 4)) for i in range(npg)]
    grid_spec = pltpu.PrefetchScalarGridSpec(
        num_scalar_prefetch=1,
        grid=(nb, nj),
        in_specs=in_specs,
        out_specs=head_blk,
        scratch_shapes=[pltpu.VMEM((N_HEADS, 1), F32),
                        pltpu.VMEM((N_HEADS, 1), F32),
                        pltpu.VMEM((N_HEADS, HEAD_DIM), F32),
                        pltpu.VMEM((1, PAGE_FLAT), F32)])
    return pl.pallas_call(
        functools.partial(_decode_kernel, n_pages=npg),
        grid_spec=grid_spec,
        out_shape=jax.ShapeDtypeStruct((nb, N_HEADS, HEAD_DIM), F32),
        compiler_params=_cparams(("arbitrary", "arbitrary")),
        name="decode_attention",
    )(page_table, q, k_new, v_new, cq_flat,
      *([cache_k] * npg), *([cache_v] * npg), *([cache_lf] * npg))


def _pool_select(s2, s4, s8, s16):
    lane = lax.broadcasted_iota(jnp.int32, (1, WIDTH_C), 1)
    g = lane // GROUP_DIM_C
    return jnp.where(g == 0, s2, jnp.where(g == 1, s4, jnp.where(g == 2, s8, s16)))


def _merge_tail(a, b_out, d, gt_ref, x, mod_ref, wa_ref, wb_ref, wc_ref, wo_ref, pw_ref, ps_ref, gp_ref):
    c_out = _bdot(d, pw_ref[...]) * ps_ref[...]
    merged = (gt_ref[:, 0:D_MODEL] * _bdot(a, wa_ref[...])
              + gt_ref[:, D_MODEL:2 * D_MODEL] * _bdot(b_out, wb_ref[...])
              + gt_ref[:, 2 * D_MODEL:] * _bdot(c_out, wc_ref[...]))
    m = _bdot(merged, wo_ref[...])
    return x + mod_ref[:, 2, :] * _rms(m, gp_ref[...])


def _mix_prompt_kernel(a_ref, bb_ref, u_ref, pc_ref, uh_ref, ph_ref, gt_ref, x_ref, mod_ref,
                       cw_ref, wa_ref, wb_ref, wc_ref, wo_ref, pw_ref, ps_ref, gp_ref,
                       o_ref, ue_sc, pe_sc, *, tm, tiles_per_seq):
    i = pl.program_id(0)
    keep = jnp.where((i % tiles_per_seq) == 0, 0.0, 1.0)
    hu, hp = SUBLANES, 2 * SUBLANES
    ue_sc[0:hu, :] = uh_ref[...] * keep
    ue_sc[hu:, :] = u_ref[...]
    pe_sc[0:hp, :] = ph_ref[...] * keep
    pe_sc[hp:, :] = pc_ref[...]
    y_conv = (ue_sc[hu - 2:hu - 2 + tm, :] * cw_ref[0:1, :]
              + ue_sc[hu - 1:hu - 1 + tm, :] * cw_ref[1:2, :]
              + u_ref[...] * cw_ref[2:3, :])
    b_out = bb_ref[...] * y_conv
    p = pc_ref[...]

    def back(jj):
        return pe_sc[hp - jj:hp - jj + tm, :]

    s2 = p + back(1)
    s4 = s2 + back(2) + back(3)
    s8 = s4 + back(4) + back(5) + back(6) + back(7)
    s16 = s8
    for jj in range(8, 16):
        s16 = s16 + back(jj)
    pos = (i % tiles_per_seq) * tm + lax.broadcasted_iota(jnp.int32, (tm, 1), 0)
    posf = (pos + 1).astype(F32)
    means = _pool_select(s2 / jnp.minimum(2.0, posf), s4 / jnp.minimum(4.0, posf),
                         s8 / jnp.minimum(8.0, posf), s16 / jnp.minimum(16.0, posf))
    d = means - p
    o_ref[...] = _merge_tail(a_ref[...], b_out, d, gt_ref, x_ref[...], mod_ref,
                             wa_ref, wb_ref, wc_ref, wo_ref, pw_ref, ps_ref, gp_ref)


def _mix_sample_kernel(a_ref, bb_ref, u_ref, pc_ref, ch_ref, ph_ref, gt_ref, x_ref, mod_ref,
                       cw_ref, wa_ref, wb_ref, wc_ref, wo_ref, pw_ref, ps_ref, gp_ref, o_ref):
    u = u_ref[...]
    y_conv = ch_ref[:, 0, :] * cw_ref[0:1, :] + ch_ref[:, 1, :] * cw_ref[1:2, :] + u * cw_ref[2:3, :]
    b_out = bb_ref[...] * y_conv
    p = pc_ref[...]
    sums = []
    acc = p
    nxt = POOL_HIST - 1
    for w in POOL_WINDOWS:
        while POOL_HIST - nxt < w:
            acc = acc + ph_ref[:, nxt, :]
            nxt -= 1
        sums.append(acc / float(w))
    d = _pool_select(*sums) - p
    o_ref[...] = _merge_tail(a_ref[...], b_out, d, gt_ref, x_ref[...], mod_ref,
                             wa_ref, wb_ref, wc_ref, wo_ref, pw_ref, ps_ref, gp_ref)


def _const_specs(shapes):
    return [pl.BlockSpec(s, lambda i, n=len(s): (0,) * n) for s in shapes]


def _mix_prompt(a, bb, u, pc, gt, x, mod, wts, tm, seq):
    t = x.shape[0]
    tiles_per_seq = seq // tm
    hu, hp = SUBLANES, 2 * SUBLANES
    row = lambda w: pl.BlockSpec((tm, w), lambda i: (i, 0))
    in_specs = [row(WIDTH_A), row(WIDTH_B), row(WIDTH_B), row(WIDTH_C),
                pl.BlockSpec((hu, WIDTH_B), lambda i: (jnp.maximum(i * (tm // hu) - 1, 0), 0)),
                pl.BlockSpec((hp, WIDTH_C), lambda i: (jnp.maximum(i * (tm // hp) - 1, 0), 0)),
                row(N_BRANCHES * D_MODEL), row(D_MODEL), _mod_spec(tm, seq)]
    in_specs += _const_specs([w.shape for w in wts])
    return pl.pallas_call(
        functools.partial(_mix_prompt_kernel, tm=tm, tiles_per_seq=tiles_per_seq),
        grid=(t // tm,),
        in_specs=in_specs,
        out_specs=row(D_MODEL),
        out_shape=jax.ShapeDtypeStruct((t, D_MODEL), F32),
        scratch_shapes=[pltpu.VMEM((hu + tm, WIDTH_B), F32), pltpu.VMEM((hp + tm, WIDTH_C), F32)],
        compiler_params=_cparams(("arbitrary",)),
        name="mix_prompt",
    )(a, bb, u, pc, u, pc, gt, x, mod, *wts)


def _mix_sample(a, bb, u, pc, conv_hist, pool_hist, gt, x, mod, wts):
    t = x.shape[0]
    shapes = [(t, WIDTH_A), (t, WIDTH_B), (t, WIDTH_B), (t, WIDTH_C), conv_hist.shape, pool_hist.shape,
              (t, N_BRANCHES * D_MODEL), (t, D_MODEL), (t, N_MOD, D_MODEL)] + [w.shape for w in wts]
    return pl.pallas_call(
        _mix_sample_kernel,
        grid=(1,),
        in_specs=_const_specs(shapes),
        out_specs=pl.BlockSpec((t, D_MODEL), lambda i: (0, 0)),
        out_shape=jax.ShapeDtypeStruct((t, D_MODEL), F32),
        compiler_params=_cparams(("arbitrary",)),
        name="mix_sample",
    )(a, bb, u, pc, conv_hist, pool_hist, gt, x, mod, *wts)


def _first_index(mask, idx, big, axis):
    return jnp.min(jnp.where(mask, idx, big), axis=axis, keepdims=True)


def _router_kernel(x_ref, mod_ref, g_ref, wr_ref, rb_ref, upper_ref,
                   h_ref, eid_ref, wt_ref, rank_ref, comb_ref, cnt_ref, carry_sc, *, tm):
    i = pl.program_id(0)

    @pl.when(i == 0)
    def _():
        carry_sc[...] = jnp.zeros(carry_sc.shape, F32)

    h = _rms(x_ref[...], g_ref[...]) * (1.0 + mod_ref[:, 4, :]) + mod_ref[:, 3, :]
    hb = h.astype(BF16)
    h_ref[...] = hb
    logits = lax.dot_general(wr_ref[...], hb, (((1,), (1,)), ((), ())),
                             preferred_element_type=F32)
    s = jax.nn.sigmoid(logits)
    sel = s + rb_ref[...]
    sg = sel.reshape(N_EXPERT_GROUPS, GROUP_SIZE, tm)
    in_idx = lax.broadcasted_iota(jnp.int32, sg.shape, 1)
    top1 = jnp.max(sg, axis=1, keepdims=True)
    f1 = _first_index(sg == top1, in_idx, GROUP_SIZE, 1)
    top2 = jnp.max(jnp.where(in_idx == f1, -jnp.inf, sg), axis=1, keepdims=True)
    gscore = top1 + top2
    g_idx = lax.broadcasted_iota(jnp.int32, gscore.shape, 0)
    gsel = jnp.zeros(gscore.shape, F32)
    for _ in range(TOPK_GROUPS):
        mx = jnp.max(gscore, axis=0, keepdims=True)
        hit = g_idx == _first_index(gscore == mx, g_idx, N_EXPERT_GROUPS, 0)
        gsel = jnp.where(hit, 1.0, gsel)
        gscore = jnp.where(hit, -jnp.inf, gscore)
    masked = jnp.where(gsel > 0.5, sg, -jnp.inf).reshape(N_EXPERTS, tm)
    e_idx = lax.broadcasted_iota(jnp.int32, masked.shape, 0)
    hits, ids, wts = [], [], []
    for _ in range(TOP_K):
        mx = jnp.max(masked, axis=0, keepdims=True)
        fe = _first_index(masked == mx, e_idx, N_EXPERTS, 0)
        hit = e_idx == fe
        hits.append(hit)
        ids.append(fe)
        wts.append(jnp.sum(jnp.where(hit, s, 0.0), axis=0, keepdims=True))
        masked = jnp.where(hit, -jnp.inf, masked)
    wsum = wts[0]
    for w in wts[1:]:
        wsum = wsum + w
    wts = [w / wsum * ROUTED_SCALE for w in wts]
    onehot = jnp.zeros((N_EXPERTS, tm), F32)
    comb = jnp.zeros((N_EXPERTS, tm), F32)
    for hit, w in zip(hits, wts):
        onehot = onehot + jnp.where(hit, 1.0, 0.0)
        comb = comb + jnp.where(hit, w, 0.0)
    before = jnp.dot(onehot.astype(BF16), upper_ref[...], preferred_element_type=F32) + carry_sc[...]
    ranks = [jnp.sum(jnp.where(hit, before, 0.0), axis=0, keepdims=True) for hit in hits]
    carry_new = carry_sc[...] + jnp.sum(onehot, axis=1, keepdims=True)
    carry_sc[...] = carry_new
    pad = jnp.zeros((SUBLANES - TOP_K, tm), F32)
    eid_ref[...] = jnp.concatenate(ids + [pad.astype(jnp.int32)], axis=0)
    wt_ref[...] = jnp.concatenate(wts + [pad], axis=0)
    rank_ref[...] = jnp.concatenate(ranks + [pad], axis=0).astype(jnp.int32)
    comb_ref[...] = comb
    cnt_ref[...] = carry_new.astype(jnp.int32)


def _router(x, mod, g, wr_t, rb, upper, tm, rows_per_mod):
    t = x.shape[0]
    col = lambda rows: pl.BlockSpec((rows, tm), lambda i: (0, i))
    return pl.pallas_call(
        functools.partial(_router_kernel, tm=tm),
        grid=(t // tm,),
        in_specs=[pl.BlockSpec((tm, D_MODEL), lambda i: (i, 0)),
                  _mod_spec(tm, rows_per_mod),
                  pl.BlockSpec((1, D_MODEL), lambda i: (0, 0)),
                  pl.BlockSpec((N_EXPERTS, D_MODEL), lambda i: (0, 0)),
                  pl.BlockSpec((N_EXPERTS, 1), lambda i: (0, 0)),
                  pl.BlockSpec((tm, tm), lambda i: (0, 0))],
        out_specs=[pl.BlockSpec((tm, D_MODEL), lambda i: (i, 0)),
                   col(SUBLANES), col(SUBLANES), col(SUBLANES), col(N_EXPERTS),
                   pl.BlockSpec((N_EXPERTS, 1), lambda i: (0, 0))],
        out_shape=[jax.ShapeDtypeStruct((t, D_MODEL), BF16),
                   jax.ShapeDtypeStruct((SUBLANES, t), jnp.int32),
                   jax.ShapeDtypeStruct((SUBLANES, t), F32),
                   jax.ShapeDtypeStruct((SUBLANES, t), jnp.int32),
                   jax.ShapeDtypeStruct((N_EXPERTS, t), F32),
                   jax.ShapeDtypeStruct((N_EXPERTS, 1), jnp.int32)],
        scratch_shapes=[pltpu.VMEM((N_EXPERTS, 1), F32)],
        compiler_params=_cparams(("arbitrary",)),
        name="router",
    )(x, mod, g, wr_t, rb, upper)


def _expert_kernel(te_ref, nv_ref, x_ref, wg_ref, wu_ref, wd_ref, y_ref, wgu_sc, wd_sc):
    t = pl.program_id(0)
    prev = te_ref[jnp.maximum(t - 1, 0)]
    fresh = jnp.logical_or(t == 0, te_ref[t] != prev)

    @pl.when(fresh)
    def _():
        wgu_sc[:, 0:D_EXPERT] = wg_ref[0, 0].astype(BF16)
        wgu_sc[:, D_EXPERT:] = wu_ref[0, 0].astype(BF16)
        wd_sc[...] = wd_ref[0, 0].astype(BF16)

    @pl.when(t < nv_ref[0])
    def _():
        gu = jnp.dot(x_ref[...], wgu_sc[...], preferred_element_type=F32)
        a = _silu(gu[:, :D_EXPERT]) * gu[:, D_EXPERT:]
        y_ref[...] = jnp.dot(a.astype(BF16), wd_sc[...], preferred_element_type=F32)


def _experts(layer, xs, tile_expert, n_valid, we_gate, we_up, we_down):
    n_tiles = xs.shape[0] // TE
    rows = lambda t, te, nv: (jnp.minimum(t, nv[0] - 1), 0)
    wmap = lambda t, te, nv: (layer, te[t], 0, 0)
    grid_spec = pltpu.PrefetchScalarGridSpec(
        num_scalar_prefetch=2,
        grid=(n_tiles,),
        in_specs=[pl.BlockSpec((TE, D_MODEL), rows),
                  pl.BlockSpec((1, 1, D_MODEL, D_EXPERT), wmap),
                  pl.BlockSpec((1, 1, D_MODEL, D_EXPERT), wmap),
                  pl.BlockSpec((1, 1, D_EXPERT, D_MODEL), wmap)],
        out_specs=pl.BlockSpec((TE, D_MODEL), rows),
        scratch_shapes=[pltpu.VMEM((D_MODEL, 2 * D_EXPERT), BF16),
                        pltpu.VMEM((D_EXPERT, D_MODEL), BF16)])
    return pl.pallas_call(
        _expert_kernel,
        grid_spec=grid_spec,
        out_shape=jax.ShapeDtypeStruct((xs.shape[0], D_MODEL), F32),
        compiler_params=_cparams(("arbitrary",)),
        name="experts",
    )(tile_expert, n_valid, xs, we_gate, we_up, we_down)


def _dense_expert_kernel(h_ref, comb_ref, wg_ref, wu_ref, wd_ref, o_ref, acc_sc):
    e = pl.program_id(0)

    @pl.when(e == 0)
    def _():
        acc_sc[...] = jnp.zeros(acc_sc.shape, F32)

    hb = h_ref[...]
    lane = lax.broadcasted_iota(jnp.int32, comb_ref.shape, 1)
    ce = jnp.sum(jnp.where(lane == e, comb_ref[...], 0.0), axis=1, keepdims=True)
    g = jnp.dot(hb, wg_ref[0, 0].astype(BF16), preferred_element_type=F32)
    u = jnp.dot(hb, wu_ref[0, 0].astype(BF16), preferred_element_type=F32)
    a = _silu(g) * u * ce
    acc_sc[...] += jnp.dot(a.astype(BF16), wd_ref[0, 0].astype(BF16), preferred_element_type=F32)

    @pl.when(e == pl.num_programs(0) - 1)
    def _():
        o_ref[...] = acc_sc[...]


def _dense_experts(layer, h, comb, we_gate, we_up, we_down):
    t = h.shape[0]
    wmap = lambda e: (layer, e, 0, 0)
    return pl.pallas_call(
        _dense_expert_kernel,
        grid=(N_EXPERTS,),
        in_specs=[pl.BlockSpec((t, D_MODEL), lambda e: (0, 0)),
                  pl.BlockSpec((t, N_EXPERTS), lambda e: (0, 0)),
                  pl.BlockSpec((1, 1, D_MODEL, D_EXPERT), wmap),
                  pl.BlockSpec((1, 1, D_MODEL, D_EXPERT), wmap),
                  pl.BlockSpec((1, 1, D_EXPERT, D_MODEL), wmap)],
        out_specs=pl.BlockSpec((t, D_MODEL), lambda e: (0, 0)),
        out_shape=jax.ShapeDtypeStruct((t, D_MODEL), F32),
        scratch_shapes=[pltpu.VMEM((t, D_MODEL), F32)],
        compiler_params=_cparams(("arbitrary",)),
        name="dense_experts",
    )(h, comb, we_gate, we_up, we_down)


def _ffn_out_kernel(x_ref, r_ref, h_ref, mod_ref, wgu_ref, wd_ref, g_ref, o_ref):
    gu = jnp.dot(h_ref[...], wgu_ref[...], preferred_element_type=F32)
    a = _silu(gu[:, :D_EXPERT]) * gu[:, D_EXPERT:]
    y = r_ref[...] + _bdot(a, wd_ref[...])
    o_ref[...] = x_ref[...] + mod_ref[:, 5, :] * _rms(y, g_ref[...])


def _ffn_out(x, routed, h, mod, ws_gu, ws_d, g, tm, rows_per_mod):
    t = x.shape[0]
    row = pl.BlockSpec((tm, D_MODEL), lambda i: (i, 0))
    return pl.pallas_call(
        _ffn_out_kernel,
        grid=(t // tm,),
        in_specs=[row, row, row, _mod_spec(tm, rows_per_mod),
                  pl.BlockSpec((D_MODEL, 2 * D_EXPERT), lambda i: (0, 0)),
                  pl.BlockSpec((D_EXPERT, D_MODEL), lambda i: (0, 0)),
                  pl.BlockSpec((1, D_MODEL), lambda i: (0, 0))],
        out_specs=row,
        out_shape=jax.ShapeDtypeStruct((t, D_MODEL), F32),
        compiler_params=_cparams(("arbitrary",)),
        name="ffn_out",
    )(x, routed, h, mod, ws_gu, ws_d, g)


def _pack_w_in(w):
    f = jnp.pad(w[:, OFF_F:OFF_BB], ((0, 0), (0, F_PAD - N_HEADS)))
    return jnp.concatenate([w[:, OFF_Q:OFF_F], w[:, OFF_BB:OFF_PC], w[:, OFF_PC:OFF_G], w[:, OFF_G:], f],
                           axis=1).astype(BF16)


def _block_diag(pool_w):
    out = jnp.zeros((WIDTH_C, WIDTH_C), F32)
    for g in range(len(POOL_WINDOWS)):
        lo = g * GROUP_DIM_C
        out = out.at[lo:lo + GROUP_DIM_C, lo:lo + GROUP_DIM_C].set(pool_w[g])
    return out.astype(BF16)


def kernel(x_prompt, x_sample, cache_k, cache_v, cache_logf, state_conv, state_pool, page_table,
           c_prompt, c_sample, w_in, b_forget, conv_w, pool_w, pool_scale, w_br_attn, w_br_conv,
           w_br_pool, w_out, g_pre_mix, g_post_mix, g_pre_ffn, g_post_ffn, w_ada, b_ada, w_router,
           router_bias, we_gate, we_up, we_down, ws_gate, ws_up, ws_down):
    bp, seq, d = x_prompt.shape
    bd = x_sample.shape[0]
    tp = bp * seq
    n_phys = cache_k.shape[1]

    xp = x_prompt.reshape(tp, d)
    xs = x_sample.reshape(bd, d)
    c_all = jnp.concatenate([c_prompt, c_sample], axis=0)
    cache_lf = cache_logf.reshape(DEPTH, n_phys, 1, PAGE_FLAT)

    tok = jnp.arange(TM)
    upper = (tok[:, None] < tok[None, :]).astype(BF16)
    tok_s = jnp.arange(bd)
    upper_s = (tok_s[:, None] < tok_s[None, :]).astype(BF16)

    n_pairs = tp * TOP_K
    n_rows_pad = n_pairs + N_EXPERTS * TE
    n_tiles = n_rows_pad // TE

    outs = {name: [] for name in ("kp", "vp", "lfp", "cvp", "plp", "ks", "vs", "lfs", "cvs", "pls")}
    for l in range(DEPTH):
        mod = _modulation(c_all, w_ada[l], b_ada[l]).reshape(bp + bd, N_MOD, d)
        mod_p, mod_s = mod[:bp], mod[bp:]
        w_pack = _pack_w_in(w_in[l])
        bf_pad = jnp.pad(b_forget[l], (0, F_PAD - N_HEADS)).reshape(1, F_PAD)
        mix_w = (conv_w[l], w_br_attn[l].astype(BF16), w_br_conv[l].astype(BF16),
                 w_br_pool[l].astype(BF16), w_out[l].astype(BF16), _block_diag(pool_w[l]),
                 pool_scale[l].reshape(1, -1), g_post_mix[l].reshape(1, -1))
        wr_t = w_router[l].T.astype(BF16)
        rb = router_bias[l].reshape(N_EXPERTS, 1)
        ws_gu = jnp.concatenate([ws_gate[l], ws_up[l]], axis=1).astype(BF16)
        ws_d = ws_down[l].astype(BF16)
        g1 = g_pre_mix[l].reshape(1, -1)
        g3 = g_pre_ffn[l].reshape(1, -1)
        g4 = g_post_ffn[l].reshape(1, -1)

        q, k, v, lfpad, bb, u, pc, gt = _in_proj(xp, mod_p, g1, w_pack, bf_pad, TM, seq)
        lf = lfpad[:, :N_HEADS].reshape(bp, seq, N_HEADS)
        c = jnp.cumsum(lf, axis=1)
        c_col = c.reshape(tp, N_HEADS // 2, 2).transpose(1, 0, 2)
        c_row = c.transpose(0, 2, 1).reshape(bp * (N_HEADS // 2), 2, seq)
        a = _prompt_attention(q, k, v, c_col, c_row, bp, seq)
        xp = _mix_prompt(a, bb, u, pc, gt, xp, mod_p, mix_w, TM, seq)
        h2, eid, wt, rank, _, counts = _router(xp, mod_p, g3, wr_t, rb, upper, TM, seq)
        counts = counts[:, 0]
        padded = ((counts + TE - 1) // TE) * TE
        ends = jnp.cumsum(padded)
        starts = ends - padded
        dest = (starts[eid[:TOP_K]] + rank[:TOP_K]).T
        tile_start = jnp.arange(n_tiles, dtype=jnp.int32) * TE
        n_valid = (ends[-1] // TE).astype(jnp.int32)
        tile_expert = jnp.searchsorted(ends, tile_start, side="right").astype(jnp.int32)
        last_expert = tile_expert[jnp.maximum(n_valid - 1, 0)]
        tile_expert = jnp.where(tile_start < ends[-1], tile_expert, last_expert)
        src = jnp.zeros((n_rows_pad,), jnp.int32).at[dest.reshape(-1)].set(
            jnp.repeat(jnp.arange(tp, dtype=jnp.int32), TOP_K), unique_indices=True)
        xs_sorted = jnp.take(h2, src, axis=0)
        y = _experts(l, xs_sorted, tile_expert, n_valid.reshape(1), we_gate, we_up, we_down)
        routed = jnp.einsum("tkd,tk->td", jnp.take(y, dest, axis=0), wt[:TOP_K].T)
        xp = _ffn_out(xp, routed, h2, mod_p, ws_gu, ws_d, g4, TM, seq)

        outs["kp"].append(k.reshape(bp, seq, N_HEADS, HEAD_DIM))
        outs["vp"].append(v.reshape(bp, seq, N_HEADS, HEAD_DIM))
        outs["lfp"].append(lf)
        outs["cvp"].append(u.reshape(bp, seq, WIDTH_B)[:, seq - (CONV_WIDTH - 1):])
        outs["plp"].append(pc.reshape(bp, seq, WIDTH_C)[:, seq - POOL_HIST:])

        q, k, v, lfpad, bb, u, pc, gt = _in_proj(xs, mod_s, g1, w_pack, bf_pad, bd, 1)
        lf = lfpad[:, :N_HEADS]
        heads = lambda z: z.reshape(bd, N_HEADS, HEAD_DIM)
        cq_flat = jnp.tile(lf, (1, PAGE_SIZE)).reshape(bd, 1, PAGE_FLAT)
        a = _decode_attention(l, heads(q), heads(k), heads(v), cq_flat,
                              cache_k, cache_v, cache_lf, page_table).reshape(bd, WIDTH_A)
        xs = _mix_sample(a, bb, u, pc, state_conv[l], state_pool[l], gt, xs, mod_s, mix_w)
        h2, _, _, _, comb, _ = _router(xs, mod_s, g3, wr_t, rb, upper_s, bd, 1)
        routed = _dense_experts(l, h2, comb.T, we_gate, we_up, we_down)
        xs = _ffn_out(xs, routed, h2, mod_s, ws_gu, ws_d, g4, bd, 1)

        outs["ks"].append(k.reshape(bd, 1, N_HEADS, HEAD_DIM))
        outs["vs"].append(v.reshape(bd, 1, N_HEADS, HEAD_DIM))
        outs["lfs"].append(lf.reshape(bd, 1, N_HEADS))
        outs["cvs"].append(jnp.concatenate([state_conv[l][:, 1:], u[:, None, :]], axis=1))
        outs["pls"].append(jnp.concatenate([state_pool[l][:, 1:], pc[:, None, :]], axis=1))

    st = {name: jnp.stack(vals) for name, vals in outs.items()}
    return (xp.reshape(bp, seq, d), xs.reshape(bd, 1, d), st["kp"], st["vp"], st["lfp"], st["cvp"],
            st["plp"], st["ks"], st["vs"], st["lfs"], st["cvs"], st["pls"])
```

```python
import functools

import jax
import jax.numpy as jnp
from jax import lax
from jax.experimental import pallas as pl
from jax.experimental.pallas import tpu as pltpu

F32 = jnp.float32
BF16 = jnp.bfloat16

D_MODEL = 1024
DEPTH = 2
PAGE_SIZE = 128
N_HEADS = 8
HEAD_DIM = 64
WIDTH_A = N_HEADS * HEAD_DIM
WIDTH_B = 256
CONV_WIDTH = 3
POOL_WINDOWS = (2, 4, 8, 16)
WIDTH_C = 256
GROUP_DIM_C = WIDTH_C // len(POOL_WINDOWS)
POOL_HIST = max(POOL_WINDOWS) - 1
N_BRANCHES = 3
OFF_Q = 0
OFF_K = OFF_Q + WIDTH_A
OFF_V = OFF_K + WIDTH_A
OFF_F = OFF_V + WIDTH_A
OFF_BB = OFF_F + N_HEADS
OFF_CB = OFF_BB + WIDTH_B
OFF_HB = OFF_CB + WIDTH_B
OFF_PC = OFF_HB + WIDTH_B
OFF_G = OFF_PC + WIDTH_C
N_IN = OFF_G + N_BRANCHES * D_MODEL
N_EXPERTS = 64
TOP_K = 6
N_EXPERT_GROUPS = 8
GROUP_SIZE = N_EXPERTS // N_EXPERT_GROUPS
TOPK_GROUPS = 4
D_EXPERT = 256
ROUTED_SCALE = 2.5
N_MOD = 6
RMS_EPS = 1e-6

LANES = 128
SUBLANES = 8
F_PAD = LANES
NEG_BIG = -1e30
VMEM_LIMIT = 56 * 1024 * 1024

C_Q, C_K, C_V = 0, 512, 1024
C_BCH = 1536
C_PC = 2304
C_G = 2560
C_F = C_G + N_BRANCHES * D_MODEL
N_PACK = C_F + F_PAD

TM = 256
TE = 256
TQ = 512
TKV = 512
NP_STEP = 8
DMA_UNROLL = 4


def _cparams(sem):
    return pltpu.CompilerParams(dimension_semantics=sem, vmem_limit_bytes=VMEM_LIMIT)


def _rms(x, g):
    ms = jnp.mean(x * x, axis=-1, keepdims=True)
    return x * lax.rsqrt(ms + RMS_EPS) * g


def _silu(x):
    return x * jax.nn.sigmoid(x)


def _log_sigmoid(x):
    return jnp.minimum(x, 0.0) - jnp.log1p(jnp.exp(-jnp.abs(x)))


def _bdot(a, b):
    return jnp.dot(a.astype(BF16), b.astype(BF16), preferred_element_type=F32)


def _mod_spec(tm, rows_per_mod):
    if rows_per_mod > 1:
        return pl.BlockSpec((1, N_MOD, D_MODEL), lambda i: (i // (rows_per_mod // tm), 0, 0))
    return pl.BlockSpec((tm, N_MOD, D_MODEL), lambda i: (i, 0, 0))


def _mod_kernel(c_ref, w_ref, b_ref, o_ref):
    o_ref[...] = _bdot(_silu(c_ref[...]), w_ref[...]) + b_ref[...]


def _modulation(c_all, w_ada, b_ada):
    n = c_all.shape[0]
    tn = 1536
    return pl.pallas_call(
        _mod_kernel,
        grid=(N_MOD * D_MODEL // tn,),
        in_specs=[pl.BlockSpec((n, D_MODEL), lambda j: (0, 0)),
                  pl.BlockSpec((D_MODEL, tn), lambda j: (0, j)),
                  pl.BlockSpec((1, tn), lambda j: (0, j))],
        out_specs=pl.BlockSpec((n, tn), lambda j: (0, j)),
        out_shape=jax.ShapeDtypeStruct((n, N_MOD * D_MODEL), F32),
        compiler_params=_cparams(("arbitrary",)),
        name="modulation",
    )(c_all, w_ada, b_ada.reshape(1, -1))


def _in_proj_kernel(x_ref, mod_ref, g_ref, w_ref, bf_ref,
                    q_ref, k_ref, v_ref, lf_ref, bb_ref, u_ref, pc_ref, gt_ref):
    h = _rms(x_ref[...], g_ref[...]) * (1.0 + mod_ref[:, 1, :]) + mod_ref[:, 0, :]
    hb = h.astype(BF16)

    def mm(lo, hi):
        return jnp.dot(hb, w_ref[:, lo:hi], preferred_element_type=F32)

    q_ref[...] = mm(C_Q, C_K)
    k_ref[...] = mm(C_K, C_V)
    v_ref[...] = mm(C_V, C_BCH)
    z = mm(C_BCH, C_PC)
    bb_ref[...] = z[:, :WIDTH_B]
    u_ref[...] = z[:, WIDTH_B:2 * WIDTH_B] * z[:, 2 * WIDTH_B:]
    pc_ref[...] = mm(C_PC, C_G)
    step = 512
    for j in range(N_BRANCHES * D_MODEL // step):
        gt_ref[:, j * step:(j + 1) * step] = jax.nn.sigmoid(mm(C_G + j * step, C_G + (j + 1) * step))
    lf_ref[...] = _log_sigmoid(mm(C_F, N_PACK) + bf_ref[...])


def _in_proj(x, mod, g, w_pack, bf_pad, tm, rows_per_mod):
    t = x.shape[0]
    widths = (WIDTH_A, WIDTH_A, WIDTH_A, F_PAD, WIDTH_B, WIDTH_B, WIDTH_C, N_BRANCHES * D_MODEL)
    return pl.pallas_call(
        _in_proj_kernel,
        grid=(t // tm,),
        in_specs=[pl.BlockSpec((tm, D_MODEL), lambda i: (i, 0)),
                  _mod_spec(tm, rows_per_mod),
                  pl.BlockSpec((1, D_MODEL), lambda i: (0, 0)),
                  pl.BlockSpec((D_MODEL, N_PACK), lambda i: (0, 0)),
                  pl.BlockSpec((1, F_PAD), lambda i: (0, 0))],
        out_specs=[pl.BlockSpec((tm, w), lambda i: (i, 0)) for w in widths],
        out_shape=[jax.ShapeDtypeStruct((t, w), F32) for w in widths],
        compiler_params=_cparams(("arbitrary",)),
        name="in_proj",
    )(x, mod, g, w_pack, bf_pad)


def _attn_kernel(q_ref, k_ref, v_ref, cq_ref, ck_ref, o_ref, m_sc, l_sc, acc_sc, *, tq, tk):
    qi = pl.program_id(2)
    kj = pl.program_id(3)
    nk = pl.num_programs(3)

    @pl.when(kj == 0)
    def _():
        m_sc[...] = jnp.full(m_sc.shape, NEG_BIG, F32)
        l_sc[...] = jnp.zeros(l_sc.shape, F32)
        acc_sc[...] = jnp.zeros(acc_sc.shape, F32)

    @pl.when(kj * tk <= qi * tq + (tq - 1))
    def _():
        lane = lax.broadcasted_iota(jnp.int32, (1, LANES), 1)
        first = lane < HEAD_DIM
        qs = q_ref[...] * (HEAD_DIM ** -0.5)
        kb = k_ref[...].astype(BF16)
        vb = v_ref[...].astype(BF16)
        rows = qi * tq + lax.broadcasted_iota(jnp.int32, (tq, 1), 0)
        cols = kj * tk + lax.broadcasted_iota(jnp.int32, (1, tk), 1)
        causal = rows >= cols
        for h in range(2):
            qh = jnp.where(first if h == 0 else jnp.logical_not(first), qs, 0.0).astype(BF16)
            s = lax.dot_general(qh, kb, (((1,), (1,)), ((), ())), preferred_element_type=F32)
            s = s + cq_ref[0, :, h:h + 1] - ck_ref[0, h:h + 1, :]
            s = jnp.where(causal, s, NEG_BIG)
            m_prev = m_sc[h]
            m_new = jnp.maximum(m_prev, jnp.max(s, axis=-1, keepdims=True))
            alpha = jnp.exp(m_prev - m_new)
            p = jnp.exp(s - m_new)
            l_sc[h] = alpha * l_sc[h] + jnp.sum(p, axis=-1, keepdims=True)
            acc_sc[h] = acc_sc[h] * alpha + jnp.dot(p.astype(BF16), vb, preferred_element_type=F32)
            m_sc[h] = m_new

    @pl.when(kj == nk - 1)
    def _():
        lane = lax.broadcasted_iota(jnp.int32, (1, LANES), 1)
        o_ref[...] = jnp.where(lane < HEAD_DIM, acc_sc[0] / l_sc[0], acc_sc[1] / l_sc[1])


def _prompt_attention(q, k, v, c_col, c_row, batch, seq):
    tq, tk = TQ, TKV
    nq, nk = seq // tq, seq // tk

    def last_kv(i):
        return (i * tq + tq - 1) // tk

    def kv_map(b, hp, i, j):
        return (b * nk + jnp.minimum(j, last_kv(i)), hp)

    return pl.pallas_call(
        functools.partial(_attn_kernel, tq=tq, tk=tk),
        grid=(batch, N_HEADS // 2, nq, nk),
        in_specs=[pl.BlockSpec((tq, LANES), lambda b, hp, i, j: (b * nq + i, hp)),
                  pl.BlockSpec((tk, LANES), kv_map),
                  pl.BlockSpec((tk, LANES), kv_map),
                  pl.BlockSpec((1, tq, 2), lambda b, hp, i, j: (hp, b * nq + i, 0)),
                  pl.BlockSpec((1, 2, tk),
                               lambda b, hp, i, j: (b * (N_HEADS // 2) + hp, 0,
                                                    jnp.minimum(j, last_kv(i))))],
        out_specs=pl.BlockSpec((tq, LANES), lambda b, hp, i, j: (b * nq + i, hp)),
        out_shape=jax.ShapeDtypeStruct((batch * seq, WIDTH_A), F32),
        scratch_shapes=[pltpu.VMEM((2, tq, 1), F32), pltpu.VMEM((2, tq, 1), F32),
                        pltpu.VMEM((2, tq, LANES), F32)],
        compiler_params=_cparams(("arbitrary",) * 4),
        name="prompt_attention",
    )(q, k, v, c_col, c_row)


def _decode_kernel(pt_ref, q_ref, kn_ref, vn_ref, cq_ref, tri_ref, *rest, n_pages):
    k_refs = rest[:n_pages]
    v_refs = rest[n_pages:2 * n_pages]
    lf_refs = rest[2 * n_pages:3 * n_pages]
    o_ref = rest[3 * n_pages]
    m_sc, l_sc, acc_sc, carry_sc = rest[3 * n_pages + 1:]
    j = pl.program_id(1)
    nj = pl.num_programs(1)
    head_of_lane = lax.broadcasted_iota(jnp.int32, (N_HEADS, WIDTH_A), 1) // HEAD_DIM
    own = head_of_lane == lax.broadcasted_iota(jnp.int32, (N_HEADS, WIDTH_A), 0)

    @pl.when(j == 0)
    def _():
        m_sc[...] = jnp.full(m_sc.shape, NEG_BIG, F32)
        l_sc[...] = jnp.zeros(l_sc.shape, F32)
        acc_sc[...] = jnp.zeros(acc_sc.shape, F32)
        carry_sc[...] = jnp.zeros(carry_sc.shape, F32)

    qbd_f = jnp.where(own, q_ref[0] * (HEAD_DIM ** -0.5), 0.0)
    qbd = qbd_f.astype(BF16)
    cq = cq_ref[0]

    lf = jnp.concatenate([r[0, 0] for r in lf_refs], axis=0)
    hi = lf.astype(BF16)
    r1 = lf - hi.astype(F32)
    mid = r1.astype(BF16)
    lo = (r1 - mid.astype(F32)).astype(BF16)
    tri = tri_ref[...]
    sums = (jnp.dot(hi, tri, preferred_element_type=F32)
            + jnp.dot(mid, tri, preferred_element_type=F32)
            + jnp.dot(lo, tri, preferred_element_type=F32))
    later = sums[:, :PAGE_SIZE]
    total = sums[:, PAGE_SIZE:]

    carry = carry_sc[...]
    scores = [None] * n_pages
    for i in reversed(range(n_pages)):
        rows = slice(i * N_HEADS, (i + 1) * N_HEADS)
        bias = later[rows] + carry + cq
        carry = carry + total[rows]
        kt = k_refs[i][0, 0].astype(BF16)
        scores[i] = jnp.dot(qbd, kt, preferred_element_type=F32) + bias
    carry_sc[...] = carry

    s_max = scores[0]
    for s in scores[1:]:
        s_max = jnp.maximum(s_max, s)
    m_prev = m_sc[...]
    m_new = jnp.maximum(m_prev, jnp.max(s_max, axis=-1, keepdims=True))
    alpha = jnp.exp(m_prev - m_new)
    acc = acc_sc[...] * alpha
    p_sum = jnp.zeros((N_HEADS, PAGE_SIZE), F32)
    for i in range(n_pages):
        p = jnp.exp(scores[i] - m_new)
        p_sum = p_sum + p
        vt = v_refs[i][0, 0].astype(BF16)
        acc = acc + lax.dot_general(p.astype(BF16), vt, (((1,), (1,)), ((), ())),
                                    preferred_element_type=F32)
    l_new = l_sc[...] * alpha + jnp.sum(p_sum, axis=-1, keepdims=True)
    m_sc[...] = m_new
    l_sc[...] = l_new
    acc_sc[...] = acc

    @pl.when(j == nj - 1)
    def _():
        s_new = jnp.sum(qbd_f * kn_ref[0], axis=-1, keepdims=True)
        m_f = jnp.maximum(m_new, s_new)
        a1 = jnp.exp(m_new - m_f)
        a2 = jnp.exp(s_new - m_f)
        o = (acc * a1 + a2 * vn_ref[0]) / (l_new * a1 + a2)
        o_ref[0] = jnp.sum(jnp.where(own, o, 0.0), axis=0, keepdims=True)


def _decode_attention(layer, q, k_new, v_new, cq, cache_kt, cache_vt, cache_lft, page_table, tri):
    nb, n_pages_total = page_table.shape
    npg = NP_STEP
    nj = n_pages_total // npg

    def page_map(i):
        def f(b, j, pt):
            return (layer, pt[b, (nj - 1 - j) * npg + i], 0, 0)
        return f

    row = lambda b, j, pt: (b, 0, 0)
    vec = pl.BlockSpec((1, 1, WIDTH_A), row)
    in_specs = [vec, vec, vec, pl.BlockSpec((1, N_HEADS, 1), row),
                pl.BlockSpec((PAGE_SIZE, 2 * PAGE_SIZE), lambda b, j, pt: (0, 0))]
    in_specs += [pl.BlockSpec((1, 1, WIDTH_A, PAGE_SIZE), page_map(i)) for i in range(npg)]
    in_specs += [pl.BlockSpec((1, 1, WIDTH_A, PAGE_SIZE), page_map(i)) for i in range(npg)]
    in_specs += [pl.BlockSpec((1, 1, N_HEADS, PAGE_SIZE), page_map(i)) for i in range(npg)]
    grid_spec = pltpu.PrefetchScalarGridSpec(
        num_scalar_prefetch=1,
        grid=(nb, nj),
        in_specs=in_specs,
        out_specs=vec,
        scratch_shapes=[pltpu.VMEM((N_HEADS, 1), F32),
                        pltpu.VMEM((N_HEADS, 1), F32),
                        pltpu.VMEM((N_HEADS, WIDTH_A), F32),
                        pltpu.VMEM((N_HEADS, PAGE_SIZE), F32)])
    return pl.pallas_call(
        functools.partial(_decode_kernel, n_pages=npg),
        grid_spec=grid_spec,
        out_shape=jax.ShapeDtypeStruct((nb, 1, WIDTH_A), F32),
        compiler_params=_cparams(("arbitrary", "arbitrary")),
        name="decode_attention",
    )(page_table, q, k_new, v_new, cq, tri,
      *([cache_kt] * npg), *([cache_vt] * npg), *([cache_lft] * npg))


def _pool_select(s2, s4, s8, s16):
    lane = lax.broadcasted_iota(jnp.int32, (1, WIDTH_C), 1)
    g = lane // GROUP_DIM_C
    return jnp.where(g == 0, s2, jnp.where(g == 1, s4, jnp.where(g == 2, s8, s16)))


def _merge_tail(a, b_out, d, gt_ref, x, mod_ref, wa_ref, wb_ref, wc_ref, wo_ref, pw_ref, ps_ref, gp_ref):
    c_out = _bdot(d, pw_ref[...]) * ps_ref[...]
    merged = (gt_ref[:, 0:D_MODEL] * _bdot(a, wa_ref[...])
              + gt_ref[:, D_MODEL:2 * D_MODEL] * _bdot(b_out, wb_ref[...])
              + gt_ref[:, 2 * D_MODEL:] * _bdot(c_out, wc_ref[...]))
    m = _bdot(merged, wo_ref[...])
    return x + mod_ref[:, 2, :] * _rms(m, gp_ref[...])


def _mix_prompt_kernel(a_ref, bb_ref, u_ref, pc_ref, uh_ref, ph_ref, gt_ref, x_ref, mod_ref,
                       cw_ref, wa_ref, wb_ref, wc_ref, wo_ref, pw_ref, ps_ref, gp_ref,
                       o_ref, ue_sc, pe_sc, *, tm, tiles_per_seq):
    i = pl.program_id(0)
    keep = jnp.where((i % tiles_per_seq) == 0, 0.0, 1.0)
    hu, hp = SUBLANES, 2 * SUBLANES
    ue_sc[0:hu, :] = uh_ref[...] * keep
    ue_sc[hu:, :] = u_ref[...]
    pe_sc[0:hp, :] = ph_ref[...] * keep
    pe_sc[hp:, :] = pc_ref[...]
    y_conv = (ue_sc[hu - 2:hu - 2 + tm, :] * cw_ref[0:1, :]
              + ue_sc[hu - 1:hu - 1 + tm, :] * cw_ref[1:2, :]
              + u_ref[...] * cw_ref[2:3, :])
    b_out = bb_ref[...] * y_conv
    p = pc_ref[...]

    def back(jj):
        return pe_sc[hp - jj:hp - jj + tm, :]

    s2 = p + back(1)
    s4 = s2 + back(2) + back(3)
    s8 = s4 + back(4) + back(5) + back(6) + back(7)
    s16 = s8
    for jj in range(8, 16):
        s16 = s16 + back(jj)
    pos = (i % tiles_per_seq) * tm + lax.broadcasted_iota(jnp.int32, (tm, 1), 0)
    posf = (pos + 1).astype(F32)
    means = _pool_select(s2 / jnp.minimum(2.0, posf), s4 / jnp.minimum(4.0, posf),
                         s8 / jnp.minimum(8.0, posf), s16 / jnp.minimum(16.0, posf))
    d = means - p
    o_ref[...] = _merge_tail(a_ref[...], b_out, d, gt_ref, x_ref[...], mod_ref,
                             wa_ref, wb_ref, wc_ref, wo_ref, pw_ref, ps_ref, gp_ref)


def _mix_sample_kernel(a_ref, bb_ref, u_ref, pc_ref, ch_ref, ph_ref, gt_ref, x_ref, mod_ref,
                       cw_ref, wa_ref, wb_ref, wc_ref, wo_ref, pw_ref, ps_ref, gp_ref, o_ref):
    u = u_ref[...]
    y_conv = ch_ref[:, 0, :] * cw_ref[0:1, :] + ch_ref[:, 1, :] * cw_ref[1:2, :] + u * cw_ref[2:3, :]
    b_out = bb_ref[...] * y_conv
    p = pc_ref[...]
    sums = []
    acc = p
    nxt = POOL_HIST - 1
    for w in POOL_WINDOWS:
        while POOL_HIST - nxt < w:
            acc = acc + ph_ref[:, nxt, :]
            nxt -= 1
        sums.append(acc / float(w))
    d = _pool_select(*sums) - p
    o_ref[...] = _merge_tail(a_ref[...], b_out, d, gt_ref, x_ref[...], mod_ref,
                             wa_ref, wb_ref, wc_ref, wo_ref, pw_ref, ps_ref, gp_ref)


def _const_specs(shapes):
    return [pl.BlockSpec(s, lambda i, n=len(s): (0,) * n) for s in shapes]


def _mix_prompt(a, bb, u, pc, gt, x, mod, wts, tm, seq):
    t = x.shape[0]
    tiles_per_seq = seq // tm
    hu, hp = SUBLANES, 2 * SUBLANES
    row = lambda w: pl.BlockSpec((tm, w), lambda i: (i, 0))
    in_specs = [row(WIDTH_A), row(WIDTH_B), row(WIDTH_B), row(WIDTH_C),
                pl.BlockSpec((hu, WIDTH_B), lambda i: (jnp.maximum(i * (tm // hu) - 1, 0), 0)),
                pl.BlockSpec((hp, WIDTH_C), lambda i: (jnp.maximum(i * (tm // hp) - 1, 0), 0)),
                row(N_BRANCHES * D_MODEL), row(D_MODEL), _mod_spec(tm, seq)]
    in_specs += _const_specs([w.shape for w in wts])
    return pl.pallas_call(
        functools.partial(_mix_prompt_kernel, tm=tm, tiles_per_seq=tiles_per_seq),
        grid=(t // tm,),
        in_specs=in_specs,
        out_specs=row(D_MODEL),
        out_shape=jax.ShapeDtypeStruct((t, D_MODEL), F32),
        scratch_shapes=[pltpu.VMEM((hu + tm, WIDTH_B), F32), pltpu.VMEM((hp + tm, WIDTH_C), F32)],
        compiler_params=_cparams(("arbitrary",)),
        name="mix_prompt",
    )(a, bb, u, pc, u, pc, gt, x, mod, *wts)


def _mix_sample(a, bb, u, pc, conv_hist, pool_hist, gt, x, mod, wts):
    t = x.shape[0]
    shapes = [(t, WIDTH_A), (t, WIDTH_B), (t, WIDTH_B), (t, WIDTH_C), conv_hist.shape, pool_hist.shape,
              (t, N_BRANCHES * D_MODEL), (t, D_MODEL), (t, N_MOD, D_MODEL)] + [w.shape for w in wts]
    return pl.pallas_call(
        _mix_sample_kernel,
        grid=(1,),
        in_specs=_const_specs(shapes),
        out_specs=pl.BlockSpec((t, D_MODEL), lambda i: (0, 0)),
        out_shape=jax.ShapeDtypeStruct((t, D_MODEL), F32),
        compiler_params=_cparams(("arbitrary",)),
        name="mix_sample",
    )(a, bb, u, pc, conv_hist, pool_hist, gt, x, mod, *wts)


def _first_index(mask, idx, big, axis):
    return jnp.min(jnp.where(mask, idx, big), axis=axis, keepdims=True)


def _router_kernel(x_ref, mod_ref, g_ref, wr_ref, rb_ref, upper_ref,
                   h_ref, eid_ref, wt_ref, rank_ref, comb_ref, cnt_ref, carry_sc, *, tm):
    i = pl.program_id(0)

    @pl.when(i == 0)
    def _():
        carry_sc[...] = jnp.zeros(carry_sc.shape, F32)

    h = _rms(x_ref[...], g_ref[...]) * (1.0 + mod_ref[:, 4, :]) + mod_ref[:, 3, :]
    h_ref[...] = h
    logits = lax.dot_general(wr_ref[...], h.astype(BF16), (((1,), (1,)), ((), ())),
                             preferred_element_type=F32)
    s = jax.nn.sigmoid(logits)
    sel = s + rb_ref[...]
    sg = sel.reshape(N_EXPERT_GROUPS, GROUP_SIZE, tm)
    in_idx = lax.broadcasted_iota(jnp.int32, sg.shape, 1)
    top1 = jnp.max(sg, axis=1, keepdims=True)
    f1 = _first_index(sg == top1, in_idx, GROUP_SIZE, 1)
    top2 = jnp.max(jnp.where(in_idx == f1, -jnp.inf, sg), axis=1, keepdims=True)
    gscore = top1 + top2
    g_idx = lax.broadcasted_iota(jnp.int32, gscore.shape, 0)
    gsel = jnp.zeros(gscore.shape, F32)
    for _ in range(TOPK_GROUPS):
        mx = jnp.max(gscore, axis=0, keepdims=True)
        hit = g_idx == _first_index(gscore == mx, g_idx, N_EXPERT_GROUPS, 0)
        gsel = jnp.where(hit, 1.0, gsel)
        gscore = jnp.where(hit, -jnp.inf, gscore)
    masked = jnp.where(gsel > 0.5, sg, -jnp.inf).reshape(N_EXPERTS, tm)
    e_idx = lax.broadcasted_iota(jnp.int32, masked.shape, 0)
    hits, ids, wts = [], [], []
    for _ in range(TOP_K):
        mx = jnp.max(masked, axis=0, keepdims=True)
        fe = _first_index(masked == mx, e_idx, N_EXPERTS, 0)
        hit = e_idx == fe
        hits.append(hit)
        ids.append(fe)
        wts.append(jnp.sum(jnp.where(hit, s, 0.0), axis=0, keepdims=True))
        masked = jnp.where(hit, -jnp.inf, masked)
    wsum = wts[0]
    for w in wts[1:]:
        wsum = wsum + w
    wts = [w / wsum * ROUTED_SCALE for w in wts]
    onehot = jnp.zeros((N_EXPERTS, tm), F32)
    comb = jnp.zeros((N_EXPERTS, tm), F32)
    for hit, w in zip(hits, wts):
        onehot = onehot + jnp.where(hit, 1.0, 0.0)
        comb = comb + jnp.where(hit, w, 0.0)
    before = jnp.dot(onehot.astype(BF16), upper_ref[...], preferred_element_type=F32) + carry_sc[...]
    ranks = [jnp.sum(jnp.where(hit, before, 0.0), axis=0, keepdims=True) for hit in hits]
    carry_new = carry_sc[...] + jnp.sum(onehot, axis=1, keepdims=True)
    carry_sc[...] = carry_new
    pad = jnp.zeros((SUBLANES - TOP_K, tm), F32)
    eid_ref[...] = jnp.concatenate(ids + [pad.astype(jnp.int32)], axis=0)
    wt_ref[...] = jnp.concatenate(wts + [pad], axis=0)
    rank_ref[...] = jnp.concatenate(ranks + [pad], axis=0).astype(jnp.int32)
    comb_ref[...] = comb
    cnt_ref[...] = carry_new.astype(jnp.int32)


def _router(x, mod, g, wr_t, rb, upper, tm, rows_per_mod):
    t = x.shape[0]
    col = lambda rows: pl.BlockSpec((rows, tm), lambda i: (0, i))
    return pl.pallas_call(
        functools.partial(_router_kernel, tm=tm),
        grid=(t // tm,),
        in_specs=[pl.BlockSpec((tm, D_MODEL), lambda i: (i, 0)),
                  _mod_spec(tm, rows_per_mod),
                  pl.BlockSpec((1, D_MODEL), lambda i: (0, 0)),
                  pl.BlockSpec((N_EXPERTS, D_MODEL), lambda i: (0, 0)),
                  pl.BlockSpec((N_EXPERTS, 1), lambda i: (0, 0)),
                  pl.BlockSpec((tm, tm), lambda i: (0, 0))],
        out_specs=[pl.BlockSpec((tm, D_MODEL), lambda i: (i, 0)),
                   col(SUBLANES), col(SUBLANES), col(SUBLANES), col(N_EXPERTS),
                   pl.BlockSpec((N_EXPERTS, 1), lambda i: (0, 0))],
        out_shape=[jax.ShapeDtypeStruct((t, D_MODEL), F32),
                   jax.ShapeDtypeStruct((SUBLANES, t), jnp.int32),
                   jax.ShapeDtypeStruct((SUBLANES, t), F32),
                   jax.ShapeDtypeStruct((SUBLANES, t), jnp.int32),
                   jax.ShapeDtypeStruct((N_EXPERTS, t), F32),
                   jax.ShapeDtypeStruct((N_EXPERTS, 1), jnp.int32)],
        scratch_shapes=[pltpu.VMEM((N_EXPERTS, 1), F32)],
        compiler_params=_cparams(("arbitrary",)),
        name="router",
    )(x, mod, g, wr_t, rb, upper)


def _row_copy_wait(src_tile, dst_tile, sem):
    pltpu.make_async_copy(src_tile, dst_tile, sem).wait()


def _dispatch_kernel(ends_ref, dest_ref, h_ref, o_ref, zero_sc, sem, *, tm):
    i = pl.program_id(0)

    def zero_tile(start):
        return pltpu.make_async_copy(zero_sc, o_ref.at[pl.ds(pl.multiple_of(start, TE), TE), :], sem)

    @pl.when(i == 0)
    def _():
        zero_sc[...] = jnp.zeros(zero_sc.shape, F32)
        n_tiles = o_ref.shape[0] // TE
        n_used = ends_ref[N_EXPERTS - 1] // TE
        for phase in ("start", "wait"):
            prev = 0
            for e in range(N_EXPERTS):
                end = ends_ref[e]

                @pl.when(end > prev)
                def _():
                    getattr(zero_tile(end - TE), phase)()
                prev = end

            def unused(t, c):
                getattr(zero_tile(t * TE), phase)()
                return c

            lax.fori_loop(n_used, n_tiles, unused, 0)

    def rows(r0, c):
        for dr in range(DMA_UNROLL):
            r = r0 * DMA_UNROLL + dr
            for k in range(TOP_K):
                d = dest_ref[0, 0, r * TOP_K + k]
                pltpu.make_async_copy(h_ref.at[pl.ds(r, 1), :], o_ref.at[pl.ds(d, 1), :], sem).start()
        return c

    lax.fori_loop(0, tm // DMA_UNROLL, rows, 0)
    for _ in range(TOP_K):
        _row_copy_wait(h_ref, o_ref.at[pl.ds(0, tm), :], sem)


def _dispatch(h, dest_tiles, ends, n_rows_pad, tm):
    t = h.shape[0]
    grid_spec = pltpu.PrefetchScalarGridSpec(
        num_scalar_prefetch=1,
        grid=(t // tm,),
        in_specs=[pl.BlockSpec((1, 1, tm * TOP_K), lambda i, ends: (i, 0, 0), memory_space=pltpu.SMEM),
                  pl.BlockSpec((tm, D_MODEL), lambda i, ends: (i, 0))],
        out_specs=pl.BlockSpec(memory_space=pl.ANY),
        scratch_shapes=[pltpu.VMEM((TE, D_MODEL), F32), pltpu.SemaphoreType.DMA(())])
    return pl.pallas_call(
        functools.partial(_dispatch_kernel, tm=tm),
        grid_spec=grid_spec,
        out_shape=jax.ShapeDtypeStruct((n_rows_pad, D_MODEL), F32),
        compiler_params=_cparams(("arbitrary",)),
        name="dispatch",
    )(ends, dest_tiles, h)


def _expert_kernel(te_ref, nv_ref, x_ref, wg_ref, wu_ref, wd_ref, y_ref, wgu_sc, wd_sc):
    t = pl.program_id(0)
    prev = te_ref[jnp.maximum(t - 1, 0)]
    fresh = jnp.logical_or(t == 0, te_ref[t] != prev)

    @pl.when(fresh)
    def _():
        wgu_sc[:, 0:D_EXPERT] = wg_ref[0, 0].astype(BF16)
        wgu_sc[:, D_EXPERT:] = wu_ref[0, 0].astype(BF16)
        wd_sc[...] = wd_ref[0, 0].astype(BF16)

    @pl.when(t < nv_ref[0])
    def _():
        gu = jnp.dot(x_ref[...].astype(BF16), wgu_sc[...], preferred_element_type=F32)
        a = _silu(gu[:, :D_EXPERT]) * gu[:, D_EXPERT:]
        y_ref[...] = jnp.dot(a.astype(BF16), wd_sc[...], preferred_element_type=F32)

    @pl.when(t >= nv_ref[0])
    def _():
        y_ref[...] = jnp.zeros(y_ref.shape, F32)


def _experts(layer, xs, tile_expert, n_valid, we_gate, we_up, we_down):
    n_tiles = xs.shape[0] // TE
    rows = lambda t, te, nv: (t, 0)
    wmap = lambda t, te, nv: (layer, te[t], 0, 0)
    grid_spec = pltpu.PrefetchScalarGridSpec(
        num_scalar_prefetch=2,
        grid=(n_tiles,),
        in_specs=[pl.BlockSpec((TE, D_MODEL), rows),
                  pl.BlockSpec((1, 1, D_MODEL, D_EXPERT), wmap),
                  pl.BlockSpec((1, 1, D_MODEL, D_EXPERT), wmap),
                  pl.BlockSpec((1, 1, D_EXPERT, D_MODEL), wmap)],
        out_specs=pl.BlockSpec((TE, D_MODEL), rows),
        scratch_shapes=[pltpu.VMEM((D_MODEL, 2 * D_EXPERT), BF16),
                        pltpu.VMEM((D_EXPERT, D_MODEL), BF16)])
    return pl.pallas_call(
        _expert_kernel,
        grid_spec=grid_spec,
        out_shape=jax.ShapeDtypeStruct((xs.shape[0], D_MODEL), F32),
        compiler_params=_cparams(("arbitrary",)),
        name="experts",
    )(tile_expert, n_valid, xs, we_gate, we_up, we_down)


def _dense_expert_kernel(h_ref, comb_ref, wg_ref, wu_ref, wd_ref, o_ref, acc_sc):
    e = pl.program_id(0)

    @pl.when(e == 0)
    def _():
        acc_sc[...] = jnp.zeros(acc_sc.shape, F32)

    hb = h_ref[...].astype(BF16)
    lane = lax.broadcasted_iota(jnp.int32, comb_ref.shape, 1)
    ce = jnp.sum(jnp.where(lane == e, comb_ref[...], 0.0), axis=1, keepdims=True)
    g = jnp.dot(hb, wg_ref[0, 0].astype(BF16), preferred_element_type=F32)
    u = jnp.dot(hb, wu_ref[0, 0].astype(BF16), preferred_element_type=F32)
    a = _silu(g) * u * ce
    acc_sc[...] += jnp.dot(a.astype(BF16), wd_ref[0, 0].astype(BF16), preferred_element_type=F32)

    @pl.when(e == pl.num_programs(0) - 1)
    def _():
        o_ref[...] = acc_sc[...]


def _dense_experts(layer, h, comb, we_gate, we_up, we_down):
    t = h.shape[0]
    wmap = lambda e: (layer, e, 0, 0)
    return pl.pallas_call(
        _dense_expert_kernel,
        grid=(N_EXPERTS,),
        in_specs=[pl.BlockSpec((t, D_MODEL), lambda e: (0, 0)),
                  pl.BlockSpec((t, N_EXPERTS), lambda e: (0, 0)),
                  pl.BlockSpec((1, 1, D_MODEL, D_EXPERT), wmap),
                  pl.BlockSpec((1, 1, D_MODEL, D_EXPERT), wmap),
                  pl.BlockSpec((1, 1, D_EXPERT, D_MODEL), wmap)],
        out_specs=pl.BlockSpec((t, D_MODEL), lambda e: (0, 0)),
        out_shape=jax.ShapeDtypeStruct((t, D_MODEL), F32),
        scratch_shapes=[pltpu.VMEM((t, D_MODEL), F32)],
        compiler_params=_cparams(("arbitrary",)),
        name="dense_experts",
    )(h, comb, we_gate, we_up, we_down)


def _ffn_tail(routed, x_ref, h_ref, mod_ref, wgu_ref, wd_ref, g_ref):
    gu = _bdot(h_ref[...], wgu_ref[...])
    a = _silu(gu[:, :D_EXPERT]) * gu[:, D_EXPERT:]
    y = routed + _bdot(a, wd_ref[...])
    return x_ref[...] + mod_ref[:, 5, :] * _rms(y, g_ref[...])


def _ffn_out_kernel(x_ref, r_ref, h_ref, mod_ref, wgu_ref, wd_ref, g_ref, o_ref):
    o_ref[...] = _ffn_tail(r_ref[...], x_ref, h_ref, mod_ref, wgu_ref, wd_ref, g_ref)


def _ffn_out(x, routed, h, mod, ws_gu, ws_d, g, tm, rows_per_mod):
    t = x.shape[0]
    row = pl.BlockSpec((tm, D_MODEL), lambda i: (i, 0))
    return pl.pallas_call(
        _ffn_out_kernel,
        grid=(t // tm,),
        in_specs=[row, row, row, _mod_spec(tm, rows_per_mod),
                  pl.BlockSpec((D_MODEL, 2 * D_EXPERT), lambda i: (0, 0)),
                  pl.BlockSpec((D_EXPERT, D_MODEL), lambda i: (0, 0)),
                  pl.BlockSpec((1, D_MODEL), lambda i: (0, 0))],
        out_specs=row,
        out_shape=jax.ShapeDtypeStruct((t, D_MODEL), F32),
        compiler_params=_cparams(("arbitrary",)),
        name="ffn_out",
    )(x, routed, h, mod, ws_gu, ws_d, g)


def _combine_kernel(dest_ref, y_ref, wt_ref, x_ref, h_ref, mod_ref, wgu_ref, wd_ref, g_ref,
                    o_ref, rows_sc, sem, *, tm):
    def rows(r0, c):
        for dr in range(DMA_UNROLL):
            r = r0 * DMA_UNROLL + dr
            for k in range(TOP_K):
                d = dest_ref[0, 0, r * TOP_K + k]
                pltpu.make_async_copy(y_ref.at[pl.ds(d, 1), :], rows_sc.at[k, pl.ds(r, 1), :], sem).start()
        return c

    lax.fori_loop(0, tm // DMA_UNROLL, rows, 0)
    for k in range(TOP_K):
        _row_copy_wait(y_ref.at[pl.ds(0, tm), :], rows_sc.at[k], sem)
    routed = rows_sc[0] * wt_ref[:, 0:1]
    for k in range(1, TOP_K):
        routed = routed + rows_sc[k] * wt_ref[:, k:k + 1]
    o_ref[...] = _ffn_tail(routed, x_ref, h_ref, mod_ref, wgu_ref, wd_ref, g_ref)


def _combine_ffn_out(x, y, dest_tiles, wt, h, mod, ws_gu, ws_d, g, tm, rows_per_mod):
    t = x.shape[0]
    row = pl.BlockSpec((tm, D_MODEL), lambda i: (i, 0))
    return pl.pallas_call(
        functools.partial(_combine_kernel, tm=tm),
        grid=(t // tm,),
        in_specs=[pl.BlockSpec((1, 1, tm * TOP_K), lambda i: (i, 0, 0), memory_space=pltpu.SMEM),
                  pl.BlockSpec(memory_space=pl.ANY),
                  pl.BlockSpec((tm, SUBLANES), lambda i: (i, 0)),
                  row, row, _mod_spec(tm, rows_per_mod),
                  pl.BlockSpec((D_MODEL, 2 * D_EXPERT), lambda i: (0, 0)),
                  pl.BlockSpec((D_EXPERT, D_MODEL), lambda i: (0, 0)),
                  pl.BlockSpec((1, D_MODEL), lambda i: (0, 0))],
        out_specs=row,
        out_shape=jax.ShapeDtypeStruct((t, D_MODEL), F32),
        scratch_shapes=[pltpu.VMEM((TOP_K, tm, D_MODEL), F32), pltpu.SemaphoreType.DMA(())],
        compiler_params=_cparams(("arbitrary",)),
        name="combine_ffn_out",
    )(dest_tiles, y, wt, x, h, mod, ws_gu, ws_d, g)


def _pack_w_in(w):
    f = jnp.pad(w[:, OFF_F:OFF_BB], ((0, 0), (0, F_PAD - N_HEADS)))
    return jnp.concatenate([w[:, OFF_Q:OFF_F], w[:, OFF_BB:OFF_PC], w[:, OFF_PC:OFF_G], w[:, OFF_G:], f],
                           axis=1).astype(BF16)


def _block_diag(pool_w):
    out = jnp.zeros((WIDTH_C, WIDTH_C), F32)
    for g in range(len(POOL_WINDOWS)):
        lo = g * GROUP_DIM_C
        out = out.at[lo:lo + GROUP_DIM_C, lo:lo + GROUP_DIM_C].set(pool_w[g])
    return out.astype(BF16)


def kernel(x_prompt, x_sample, cache_k, cache_v, cache_logf, state_conv, state_pool, page_table,
           c_prompt, c_sample, w_in, b_forget, conv_w, pool_w, pool_scale, w_br_attn, w_br_conv,
           w_br_pool, w_out, g_pre_mix, g_post_mix, g_pre_ffn, g_post_ffn, w_ada, b_ada, w_router,
           router_bias, we_gate, we_up, we_down, ws_gate, ws_up, ws_down):
    bp, seq, d = x_prompt.shape
    bd = x_sample.shape[0]
    tp = bp * seq
    n_phys = cache_k.shape[1]

    xp = x_prompt.reshape(tp, d)
    xs = x_sample.reshape(bd, d)
    c_all = jnp.concatenate([c_prompt, c_sample], axis=0)
    cache_kt = cache_k.transpose(0, 1, 3, 4, 2).reshape(DEPTH, n_phys, WIDTH_A, PAGE_SIZE)
    cache_vt = cache_v.transpose(0, 1, 3, 4, 2).reshape(DEPTH, n_phys, WIDTH_A, PAGE_SIZE)
    cache_lft = cache_logf.transpose(0, 1, 3, 2)

    pos = jnp.arange(PAGE_SIZE)
    tri = jnp.concatenate([(pos[:, None] > pos[None, :]).astype(BF16),
                           jnp.ones((PAGE_SIZE, PAGE_SIZE), BF16)], axis=1)
    tok = jnp.arange(TM)
    upper = (tok[:, None] < tok[None, :]).astype(BF16)
    tok_s = jnp.arange(bd)
    upper_s = (tok_s[:, None] < tok_s[None, :]).astype(BF16)

    n_pairs = tp * TOP_K
    n_rows_pad = n_pairs + N_EXPERTS * TE
    n_tiles = n_rows_pad // TE
    experts_iota = jnp.arange(N_EXPERTS, dtype=jnp.int32)

    outs = {name: [] for name in ("kp", "vp", "lfp", "cvp", "plp", "ks", "vs", "lfs", "cvs", "pls")}
    for l in range(DEPTH):
        mod = _modulation(c_all, w_ada[l], b_ada[l]).reshape(bp + bd, N_MOD, d)
        mod_p, mod_s = mod[:bp], mod[bp:]
        w_pack = _pack_w_in(w_in[l])
        bf_pad = jnp.pad(b_forget[l], (0, F_PAD - N_HEADS)).reshape(1, F_PAD)
        mix_w = (conv_w[l], w_br_attn[l].astype(BF16), w_br_conv[l].astype(BF16),
                 w_br_pool[l].astype(BF16), w_out[l].astype(BF16), _block_diag(pool_w[l]),
                 pool_scale[l].reshape(1, -1), g_post_mix[l].reshape(1, -1))
        wr_t = w_router[l].T.astype(BF16)
        rb = router_bias[l].reshape(N_EXPERTS, 1)
        ws_gu = jnp.concatenate([ws_gate[l], ws_up[l]], axis=1).astype(BF16)
        ws_d = ws_down[l].astype(BF16)
        g1 = g_pre_mix[l].reshape(1, -1)
        g3 = g_pre_ffn[l].reshape(1, -1)
        g4 = g_post_ffn[l].reshape(1, -1)

        q, k, v, lfpad, bb, u, pc, gt = _in_proj(xp, mod_p, g1, w_pack, bf_pad, TM, seq)
        lf = lfpad[:, :N_HEADS].reshape(bp, seq, N_HEADS)
        c = jnp.cumsum(lf, axis=1)
        c_col = c.reshape(tp, N_HEADS // 2, 2).transpose(1, 0, 2)
        c_row = c.transpose(0, 2, 1).reshape(bp * (N_HEADS // 2), 2, seq)
        a = _prompt_attention(q, k, v, c_col, c_row, bp, seq)
        xp = _mix_prompt(a, bb, u, pc, gt, xp, mod_p, mix_w, TM, seq)
        h2, eid, wt, rank, _, counts = _router(xp, mod_p, g3, wr_t, rb, upper, TM, seq)
        counts = counts[:, 0]
        padded = ((counts + TE - 1) // TE) * TE
        ends = jnp.cumsum(padded)
        starts = ends - padded
        eid6, rank6 = eid[:TOP_K], rank[:TOP_K]
        start_of = jnp.sum(jnp.where(eid6[:, :, None] == experts_iota, starts, 0), axis=-1)
        dest_tiles = (start_of + rank6).T.reshape(tp // TM, 1, TM * TOP_K)
        tile_start = jnp.arange(n_tiles, dtype=jnp.int32) * TE
        n_valid = ends[-1] // TE
        tile_expert = jnp.sum((tile_start[:, None] >= ends[None, :]).astype(jnp.int32), axis=1)
        last_expert = jnp.sum((ends[-1] - TE >= ends).astype(jnp.int32))
        tile_expert = jnp.minimum(tile_expert, last_expert)
        xs_sorted = _dispatch(h2, dest_tiles, ends, n_rows_pad, TM)
        y = _experts(l, xs_sorted, tile_expert, n_valid.reshape(1), we_gate, we_up, we_down)
        xp = _combine_ffn_out(xp, y, dest_tiles, wt.T, h2, mod_p, ws_gu, ws_d, g4, TM, seq)

        outs["kp"].append(k.reshape(bp, seq, N_HEADS, HEAD_DIM))
        outs["vp"].append(v.reshape(bp, seq, N_HEADS, HEAD_DIM))
        outs["lfp"].append(lf)
        outs["cvp"].append(u.reshape(bp, seq, WIDTH_B)[:, seq - (CONV_WIDTH - 1):])
        outs["plp"].append(pc.reshape(bp, seq, WIDTH_C)[:, seq - POOL_HIST:])

        q, k, v, lfpad, bb, u, pc, gt = _in_proj(xs, mod_s, g1, w_pack, bf_pad, bd, 1)
        lf = lfpad[:, :N_HEADS]
        vec = lambda z: z.reshape(bd, 1, WIDTH_A)
        a = _decode_attention(l, vec(q), vec(k), vec(v), lf.reshape(bd, N_HEADS, 1),
                              cache_kt, cache_vt, cache_lft, page_table, tri).reshape(bd, WIDTH_A)
        xs = _mix_sample(a, bb, u, pc, state_conv[l], state_pool[l], gt, xs, mod_s, mix_w)
        h2, _, _, _, comb, _ = _router(xs, mod_s, g3, wr_t, rb, upper_s, bd, 1)
        routed = _dense_experts(l, h2, comb.T, we_gate, we_up, we_down)
        xs = _ffn_out(xs, routed, h2, mod_s, ws_gu, ws_d, g4, bd, 1)

        outs["ks"].append(k.reshape(bd, 1, N_HEADS, HEAD_DIM))
        outs["vs"].append(v.reshape(bd, 1, N_HEADS, HEAD_DIM))
        outs["lfs"].append(lf.reshape(bd, 1, N_HEADS))
        outs["cvs"].append(jnp.concatenate([state_conv[l][:, 1:], u[:, None, :]], axis=1))
        outs["pls"].append(jnp.concatenate([state_pool[l][:, 1:], pc[:, None, :]], axis=1))

    st = {name: jnp.stack(vals) for name, vals in outs.items()}
    return (xp.reshape(bp, seq, d), xs.reshape(bd, 1, d), st["kp"], st["vp"], st["lfp"], st["cvp"],
            st["plp"], st["ks"], st["vs"], st["lfs"], st["cvs"], st["pls"])
```

```python
import functools

import jax
import jax.numpy as jnp
import numpy as np
from jax import lax
from jax.experimental import pallas as pl
from jax.experimental.pallas import tpu as pltpu

F32 = jnp.float32
BF16 = jnp.bfloat16

D_MODEL = 1024
DEPTH = 2
PAGE_SIZE = 128
N_HEADS = 8
HEAD_DIM = 64
WIDTH_A = N_HEADS * HEAD_DIM
WIDTH_B = 256
CONV_WIDTH = 3
POOL_WINDOWS = (2, 4, 8, 16)
WIDTH_C = 256
GROUP_DIM_C = WIDTH_C // len(POOL_WINDOWS)
POOL_HIST = max(POOL_WINDOWS) - 1
N_BRANCHES = 3
OFF_Q = 0
OFF_K = OFF_Q + WIDTH_A
OFF_V = OFF_K + WIDTH_A
OFF_F = OFF_V + WIDTH_A
OFF_BB = OFF_F + N_HEADS
OFF_CB = OFF_BB + WIDTH_B
OFF_HB = OFF_CB + WIDTH_B
OFF_PC = OFF_HB + WIDTH_B
OFF_G = OFF_PC + WIDTH_C
N_IN = OFF_G + N_BRANCHES * D_MODEL
N_EXPERTS = 64
TOP_K = 6
N_EXPERT_GROUPS = 8
GROUP_SIZE = N_EXPERTS // N_EXPERT_GROUPS
TOPK_GROUPS = 4
D_EXPERT = 256
ROUTED_SCALE = 2.5
N_MOD = 6
RMS_EPS = 1e-6

LANES = 128
SUBLANES = 8
F_PAD = LANES
NEG_BIG = -1e30
VMEM_LIMIT = 56 * 1024 * 1024

C_Q, C_K, C_V = 0, 512, 1024
C_BCH = 1536
C_PC = 2304
C_G = 2560
C_F = C_G + N_BRANCHES * D_MODEL
N_PACK = C_F + F_PAD

TM = 256
TE = 512
TQ = 512
TKV = 512
NP_STEP = 8
DMA_UNROLL = 4


def _cparams(sem):
    return pltpu.CompilerParams(dimension_semantics=sem, vmem_limit_bytes=VMEM_LIMIT)


def _rms(x, g):
    ms = jnp.mean(x * x, axis=-1, keepdims=True)
    return x * lax.rsqrt(ms + RMS_EPS) * g


def _silu(x):
    return x * jax.nn.sigmoid(x)


def _log_sigmoid(x):
    return jnp.minimum(x, 0.0) - jnp.log1p(jnp.exp(-jnp.abs(x)))


def _bdot(a, b):
    return jnp.dot(a.astype(BF16), b.astype(BF16), preferred_element_type=F32)


def _mod_spec(tm, rows_per_mod):
    if rows_per_mod > 1:
        return pl.BlockSpec((1, N_MOD, D_MODEL), lambda i: (i // (rows_per_mod // tm), 0, 0))
    return pl.BlockSpec((tm, N_MOD, D_MODEL), lambda i: (i, 0, 0))


def _mod_kernel(c_ref, w_ref, b_ref, o_ref):
    o_ref[...] = _bdot(_silu(c_ref[...]), w_ref[...]) + b_ref[...]


def _modulation(c_all, w_ada, b_ada):
    n = c_all.shape[0]
    tn = 1536
    return pl.pallas_call(
        _mod_kernel,
        grid=(N_MOD * D_MODEL // tn,),
        in_specs=[pl.BlockSpec((n, D_MODEL), lambda j: (0, 0)),
                  pl.BlockSpec((D_MODEL, tn), lambda j: (0, j)),
                  pl.BlockSpec((1, tn), lambda j: (0, j))],
        out_specs=pl.BlockSpec((n, tn), lambda j: (0, j)),
        out_shape=jax.ShapeDtypeStruct((n, N_MOD * D_MODEL), F32),
        compiler_params=_cparams(("arbitrary",)),
        name="modulation",
    )(c_all, w_ada, b_ada.reshape(1, -1))


def _in_proj_kernel(x_ref, mod_ref, g_ref, w_ref, bf_ref,
                    q_ref, k_ref, v_ref, lf_ref, bb_ref, u_ref, pc_ref, gt_ref):
    h = _rms(x_ref[...], g_ref[...]) * (1.0 + mod_ref[:, 1, :]) + mod_ref[:, 0, :]
    hb = h.astype(BF16)

    def mm(lo, hi):
        return jnp.dot(hb, w_ref[:, lo:hi], preferred_element_type=F32)

    q_ref[...] = mm(C_Q, C_K)
    k_ref[...] = mm(C_K, C_V)
    v_ref[...] = mm(C_V, C_BCH)
    z = mm(C_BCH, C_PC)
    bb_ref[...] = z[:, :WIDTH_B]
    u_ref[...] = z[:, WIDTH_B:2 * WIDTH_B] * z[:, 2 * WIDTH_B:]
    pc_ref[...] = mm(C_PC, C_G)
    step = 512
    for j in range(N_BRANCHES * D_MODEL // step):
        gt_ref[:, j * step:(j + 1) * step] = jax.nn.sigmoid(mm(C_G + j * step, C_G + (j + 1) * step))
    lf_ref[...] = _log_sigmoid(mm(C_F, N_PACK) + bf_ref[...])


def _in_proj(x, mod, g, w_pack, bf_pad, tm, rows_per_mod):
    t = x.shape[0]
    widths = (WIDTH_A, WIDTH_A, WIDTH_A, F_PAD, WIDTH_B, WIDTH_B, WIDTH_C, N_BRANCHES * D_MODEL)
    return pl.pallas_call(
        _in_proj_kernel,
        grid=(t // tm,),
        in_specs=[pl.BlockSpec((tm, D_MODEL), lambda i: (i, 0)),
                  _mod_spec(tm, rows_per_mod),
                  pl.BlockSpec((1, D_MODEL), lambda i: (0, 0)),
                  pl.BlockSpec((D_MODEL, N_PACK), lambda i: (0, 0)),
                  pl.BlockSpec((1, F_PAD), lambda i: (0, 0))],
        out_specs=[pl.BlockSpec((tm, w), lambda i: (i, 0)) for w in widths],
        out_shape=[jax.ShapeDtypeStruct((t, w), F32) for w in widths],
        compiler_params=_cparams(("arbitrary",)),
        name="in_proj",
    )(x, mod, g, w_pack, bf_pad)


LOG2E = 1.4426950408889634
N_PIECES = 3


def _split3(x):
    hi = x.astype(BF16).astype(F32)
    r = x - hi
    mid = r.astype(BF16).astype(F32)
    lo = (r - mid).astype(BF16).astype(F32)
    return hi, mid, lo


def _forget_bias_kernel(lf_ref, tril_ref, pq_ref, pk_ref, oq_ref, ok_ref, exq_ref, exk_ref):
    tril = tril_ref[...]
    c = None
    for piece in _split3(lf_ref[...]):
        part = jnp.dot(tril, piece.astype(BF16), preferred_element_type=F32)
        c = part if c is None else c + part
    pieces = jnp.concatenate([p.astype(BF16) for p in _split3(c * LOG2E)], axis=1)
    for hp in range(N_HEADS // 2):
        exq_ref[hp] = jnp.dot(pieces, pq_ref[hp], preferred_element_type=F32) + oq_ref[...]
        exk_ref[hp] = jnp.dot(pieces, pk_ref[hp], preferred_element_type=F32) + ok_ref[...]


def _bias_placement():
    pq =np.zeros((N_HEADS // 2, N_PIECES * LANES, LANES), np.float32)
    pk = np.zeros_like(pq)
    oq = np.zeros((1, LANES), np.float32)
    ok = np.zeros((1, LANES), np.float32)
    for hp in range(N_HEADS // 2):
        for side in range(2):
            spare = HEAD_DIM * (1 - side)
            for piece in range(N_PIECES):
                src = piece * LANES + 2 * hp + side
                pq[hp, src, spare + piece] = 1.0
                pk[hp, src, spare + N_PIECES + piece] = -1.0
    for side in range(2):
        spare = HEAD_DIM * (1 - side)
        oq[0, spare + N_PIECES:spare + 2 * N_PIECES] = 1.0
        ok[0, spare:spare + N_PIECES] = 1.0
    return (jnp.asarray(pq, BF16), jnp.asarray(pk, BF16), jnp.asarray(oq), jnp.asarray(ok))


def _forget_bias(lfpad, batch, seq):
    pos = jnp.arange(seq)
    tril = (pos[:, None] >= pos[None, :]).astype(BF16)
    pq, pk, oq, ok = _bias_placement()
    n_pairs = N_HEADS // 2
    const = lambda shape: pl.BlockSpec(shape, lambda b, n=len(shape): (0,) * n)
    out_spec = pl.BlockSpec((n_pairs, seq, LANES), lambda b: (0, b, 0))
    out_shape = jax.ShapeDtypeStruct((n_pairs, batch * seq, LANES), F32)
    return pl.pallas_call(
        _forget_bias_kernel,
        grid=(batch,),
        in_specs=[pl.BlockSpec((seq, LANES), lambda b: (b, 0)),
                  const((seq, seq)), const(pq.shape), const(pk.shape), const(oq.shape), const(ok.shape)],
        out_specs=[out_spec, out_spec],
        out_shape=[out_shape, out_shape],
        compiler_params=_cparams(("arbitrary",)),
        name="forget_bias",
    )(lfpad, tril, pq, pk, oq, ok)


def _attn_kernel(q_ref, k_ref, v_ref, exq_ref, exk_ref, o_ref, qa_sc, m_sc, l_sc, acc_sc, *, tq, tk):
    qi = pl.program_id(2)
    kj = pl.program_id(3)
    nk = pl.num_programs(3)
    lane = lax.broadcasted_iota(jnp.int32, (1, LANES), 1)
    first = lane < HEAD_DIM

    @pl.when(kj == 0)
    def _():
        qs = q_ref[...] * (HEAD_DIM ** -0.5 * LOG2E)
        ex = exq_ref[0]
        qa_sc[0] = jnp.where(first, qs, ex).astype(BF16)
        qa_sc[1] = jnp.where(first, ex, qs).astype(BF16)
        m_sc[...] = jnp.full(m_sc.shape, NEG_BIG, F32)
        l_sc[...] = jnp.zeros(l_sc.shape, F32)
        acc_sc[...] = jnp.zeros(acc_sc.shape, F32)

    def block(masked):
        kx = k_ref[...]
        ex = exk_ref[0]
        vb = v_ref[...].astype(BF16)
        if masked:
            key_pos = kj * tk + lax.broadcasted_iota(jnp.int32, (tk, 1), 0)
            q_pos = qi * tq + lax.broadcasted_iota(jnp.int32, (1, tq), 1)
            causal = key_pos <= q_pos
        for h in range(2):
            ka = (jnp.where(first, kx, ex) if h == 0 else jnp.where(first, ex, kx)).astype(BF16)
            s = lax.dot_general(ka, qa_sc[h], (((1,), (1,)), ((), ())),
                                preferred_element_type=F32)
            if masked:
                s = jnp.where(causal, s, NEG_BIG)
            m_prev = m_sc[h]
            m_new = jnp.maximum(m_prev, jnp.max(s, axis=0, keepdims=True))
            alpha = jnp.exp2(m_prev - m_new)
            p = jnp.exp2(s - m_new)
            l_sc[h] = alpha * l_sc[h] + jnp.sum(p, axis=0, keepdims=True)
            pv = lax.dot_general(vb, p.astype(BF16), (((0,), (0,)), ((), ())),
                                 preferred_element_type=F32)
            acc_sc[h] = acc_sc[h] * alpha + pv
            m_sc[h] = m_new

    @pl.when(kj * tk + (tk - 1) <= qi * tq)
    def _():
        block(False)

    @pl.when(jnp.logical_and(kj * tk + (tk - 1) > qi * tq, kj * tk <= qi * tq + (tq - 1)))
    def _():
        block(True)

    @pl.when(kj == nk - 1)
    def _():
        row = lax.broadcasted_iota(jnp.int32, (LANES, 1), 0)
        o_t = jnp.where(row < HEAD_DIM, acc_sc[0] / l_sc[0], acc_sc[1] / l_sc[1])
        o_ref[...] = o_t.T


def _prompt_attention(q, k, v, exq, exk, batch, seq):
    tq, tk = TQ, TKV
    nq, nk = seq // tq, seq // tk

    def last_kv(i):
        return (i * tq + tq - 1) // tk

    def kv_map(b, hp, i, j):
        return (b * nk + jnp.minimum(j, last_kv(i)), hp)

    return pl.pallas_call(
        functools.partial(_attn_kernel, tq=tq, tk=tk),
        grid=(batch, N_HEADS // 2, nq, nk),
        in_specs=[pl.BlockSpec((tq, LANES), lambda b, hp, i, j: (b * nq + i, hp)),
                  pl.BlockSpec((tk, LANES), kv_map),
                  pl.BlockSpec((tk, LANES), kv_map),
                  pl.BlockSpec((1, tq, LANES), lambda b, hp, i, j: (hp, b * nq + i, 0)),
                  pl.BlockSpec((1, tk, LANES),
                               lambda b, hp, i, j: (hp, b * nk + jnp.minimum(j, last_kv(i)), 0))],
        out_specs=pl.BlockSpec((tq, LANES), lambda b, hp, i, j: (b * nq + i, hp)),
        out_shape=jax.ShapeDtypeStruct((batch * seq, WIDTH_A), F32),
        scratch_shapes=[pltpu.VMEM((2, tq, LANES), BF16),
                        pltpu.VMEM((2, 1, tq), F32), pltpu.VMEM((2, 1, tq), F32),
                        pltpu.VMEM((2, LANES, tq), F32)],
        compiler_params=_cparams(("arbitrary",) * 4),
        name="prompt_attention",
    )(q, k, v, exq, exk)


def _zero_fill_padding(ends_ref, xs_ref, zero_sc, sem):
    def zero_tile(start):
        return pltpu.make_async_copy(zero_sc, xs_ref.at[pl.ds(pl.multiple_of(start, TE), TE), :], sem)

    zero_sc[...] = jnp.zeros(zero_sc.shape, F32)
    n_tiles = xs_ref.shape[0] // TE
    n_used = ends_ref[N_EXPERTS - 1] // TE
    for phase in ("start", "wait"):
        prev = 0
        for e in range(N_EXPERTS):
            end = ends_ref[e]

            @pl.when(end > prev)
            def _():
                getattr(zero_tile(end - TE), phase)()
            prev = end

        def unused(t, c):
            getattr(zero_tile(t * TE), phase)()
            return c

        lax.fori_loop(n_used, n_tiles, unused, 0)


def _decode_kernel(pt_ref, ends_ref, q_ref, kn_ref, vn_ref, cq_ref, tri_ref, dest_ref, h_ref, *rest,
                   n_pages, rows_per_step):
    k_refs = rest[:n_pages]
    v_refs = rest[n_pages:2 * n_pages]
    lf_refs = rest[2 * n_pages:3 * n_pages]
    o_ref, xs_ref = rest[3 * n_pages:3 * n_pages + 2]
    m_sc, l_sc, acc_sc, carry_sc, zero_sc, sem = rest[3 * n_pages + 2:]
    j = pl.program_id(1)
    nj = pl.num_programs(1)
    step = pl.program_id(0) * nj + j
    head_of_lane = lax.broadcasted_iota(jnp.int32, (N_HEADS, WIDTH_A), 1) // HEAD_DIM
    own = head_of_lane == lax.broadcasted_iota(jnp.int32, (N_HEADS, WIDTH_A), 0)

    @pl.when(step == 0)
    def _():
        _zero_fill_padding(ends_ref, xs_ref, zero_sc, sem)

    tok0 = step * (rows_per_step // TOP_K)
    for idx in range(rows_per_step):
        d = dest_ref[0, 0, idx]
        pltpu.make_async_copy(h_ref.at[pl.ds(tok0 + idx // TOP_K, 1), :],
                              xs_ref.at[pl.ds(d, 1), :], sem).start()

    @pl.when(j == 0)
    def _():
        m_sc[...] = jnp.full(m_sc.shape, NEG_BIG, F32)
        l_sc[...] = jnp.zeros(l_sc.shape, F32)
        acc_sc[...] = jnp.zeros(acc_sc.shape, F32)
        carry_sc[...] = jnp.zeros(carry_sc.shape, F32)

    qbd_f = jnp.where(own, q_ref[0] * (HEAD_DIM ** -0.5), 0.0)
    qbd = qbd_f.astype(BF16)
    cq = cq_ref[0]

    lf = jnp.concatenate([r[0, 0] for r in lf_refs], axis=0)
    hi = lf.astype(BF16)
    r1 = lf - hi.astype(F32)
    mid = r1.astype(BF16)
    lo = (r1 - mid.astype(F32)).astype(BF16)
    tri = tri_ref[...]
    sums = (jnp.dot(hi, tri, preferred_element_type=F32)
            + jnp.dot(mid, tri, preferred_element_type=F32)
            + jnp.dot(lo, tri, preferred_element_type=F32))
    later = sums[:, :PAGE_SIZE]
    total = sums[:, PAGE_SIZE:]

    carry = carry_sc[...]
    bias = [None] * n_pages
    for i in reversed(range(n_pages)):
        rows = slice(i * N_HEADS, (i + 1) * N_HEADS)
        bias[i] = later[rows] + carry + cq
        carry = carry + total[rows]
    carry_sc[...] = carry

    def pair(refs, g):
        return jnp.concatenate([refs[2 * g][0, 0], refs[2 * g + 1][0, 0]], axis=1).astype(BF16)

    n_pairs = n_pages // 2
    scores = [jnp.dot(qbd, pair(k_refs, g), preferred_element_type=F32)
              + jnp.concatenate([bias[2 * g], bias[2 * g + 1]], axis=1) for g in range(n_pairs)]
    s_max = scores[0]
    for s in scores[1:]:
        s_max = jnp.maximum(s_max, s)
    m_prev = m_sc[...]
    m_new = jnp.maximum(m_prev, jnp.max(s_max, axis=-1, keepdims=True))
    alpha = jnp.exp(m_prev - m_new)
    acc = acc_sc[...] * alpha
    p_sum = jnp.zeros((N_HEADS, 2 * PAGE_SIZE), F32)
    for g in range(n_pairs):
        p = jnp.exp(scores[g] - m_new)
        p_sum = p_sum + p
        acc = acc + lax.dot_general(p.astype(BF16), pair(v_refs, g), (((1,), (1,)), ((), ())),
                                    preferred_element_type=F32)
    l_new = l_sc[...] * alpha + jnp.sum(p_sum, axis=-1, keepdims=True)
    m_sc[...] = m_new
    l_sc[...] = l_new
    acc_sc[...] = acc

    @pl.when(j == nj - 1)
    def _():
        s_new = jnp.sum(qbd_f * kn_ref[0], axis=-1, keepdims=True)
        m_f = jnp.maximum(m_new, s_new)
        a1 = jnp.exp(m_new - m_f)
        a2 = jnp.exp(s_new - m_f)
        o = (acc * a1 + a2 * vn_ref[0]) / (l_new * a1 + a2)
        o_ref[0] = jnp.sum(jnp.where(own, o, 0.0), axis=0, keepdims=True)

    pltpu.make_async_copy(h_ref.at[pl.ds(0, rows_per_step), :],
                          xs_ref.at[pl.ds(0, rows_per_step), :], sem).wait()


def _decode_attention_dispatch(layer, q, k_new, v_new, cq, cache_kt, cache_vt, cache_lft, page_table, tri,
                               h_prompt, dest, ends, n_rows_pad):
    nb, n_pages_total = page_table.shape
    npg = NP_STEP
    nj = n_pages_total // npg
    n_steps = nb * nj
    rows_per_step = dest.size // n_steps
    assert rows_per_step * n_steps == dest.size and rows_per_step % TOP_K == 0
    dest_steps = dest.reshape(n_steps, 1, rows_per_step)

    def page_map(i):
        def f(b, j, pt, ends):
            return (layer, pt[b, (nj - 1 - j) * npg + i], 0, 0)
        return f

    row = lambda b, j, pt, ends: (b, 0, 0)
    vec = pl.BlockSpec((1, 1, WIDTH_A), row)
    in_specs = [vec, vec, vec, pl.BlockSpec((1, N_HEADS, 1), row),
                pl.BlockSpec((PAGE_SIZE, 2 * PAGE_SIZE), lambda b, j, pt, ends: (0, 0)),
                pl.BlockSpec((1, 1, rows_per_step), lambda b, j, pt, ends: (b * nj + j, 0, 0),
                             memory_space=pltpu.SMEM),
                pl.BlockSpec(memory_space=pl.ANY)]
    in_specs += [pl.BlockSpec((1, 1, WIDTH_A, PAGE_SIZE), page_map(i)) for i in range(npg)]
    in_specs += [pl.BlockSpec((1, 1, WIDTH_A, PAGE_SIZE), page_map(i)) for i in range(npg)]
    in_specs += [pl.BlockSpec((1, 1, N_HEADS, PAGE_SIZE), page_map(i)) for i in range(npg)]
    grid_spec = pltpu.PrefetchScalarGridSpec(
        num_scalar_prefetch=2,
        grid=(nb, nj),
        in_specs=in_specs,
        out_specs=[vec, pl.BlockSpec(memory_space=pl.ANY)],
        scratch_shapes=[pltpu.VMEM((N_HEADS, 1), F32),
                        pltpu.VMEM((N_HEADS, 1), F32),
                        pltpu.VMEM((N_HEADS, WIDTH_A), F32),
                        pltpu.VMEM((N_HEADS, PAGE_SIZE), F32),
                        pltpu.VMEM((TE, D_MODEL), F32),
                        pltpu.SemaphoreType.DMA(())])
    return pl.pallas_call(
        functools.partial(_decode_kernel, n_pages=npg, rows_per_step=rows_per_step),
        grid_spec=grid_spec,
        out_shape=[jax.ShapeDtypeStruct((nb, 1, WIDTH_A), F32),
                   jax.ShapeDtypeStruct((n_rows_pad, D_MODEL), F32)],
        compiler_params=_cparams(("arbitrary", "arbitrary")),
        name="decode_attention_dispatch",
    )(page_table, ends, q, k_new, v_new, cq, tri, dest_steps, h_prompt,
      *([cache_kt] * npg), *([cache_vt] * npg), *([cache_lft] * npg))


def _pool_select(s2, s4, s8, s16):
    lane = lax.broadcasted_iota(jnp.int32, (1, WIDTH_C), 1)
    g = lane // GROUP_DIM_C
    return jnp.where(g == 0, s2, jnp.where(g == 1, s4, jnp.where(g == 2, s8, s16)))


def _merge_tail(a, b_out, d, gt_ref, x, mod_ref, wa_ref, wb_ref, wc_ref, wo_ref, pw_ref, ps_ref, gp_ref):
    c_out = _bdot(d, pw_ref[...]) * ps_ref[...]
    merged = (gt_ref[:, 0:D_MODEL] * _bdot(a, wa_ref[...])
              + gt_ref[:, D_MODEL:2 * D_MODEL] * _bdot(b_out, wb_ref[...])
              + gt_ref[:, 2 * D_MODEL:] * _bdot(c_out, wc_ref[...]))
    m = _bdot(merged, wo_ref[...])
    return x + mod_ref[:, 2, :] * _rms(m, gp_ref[...])


def _mix_prompt_kernel(a_ref, bb_ref, u_ref, pc_ref, uh_ref, ph_ref, gt_ref, x_ref, mod_ref,
                       cw_ref, wa_ref, wb_ref, wc_ref, wo_ref, pw_ref, ps_ref, gp_ref,
                       o_ref, ue_sc, pe_sc, *, tm, tiles_per_seq):
    i = pl.program_id(0)
    keep = jnp.where((i % tiles_per_seq) == 0, 0.0, 1.0)
    hu, hp = SUBLANES, 2 * SUBLANES
    ue_sc[0:hu, :] = uh_ref[...] * keep
    ue_sc[hu:, :] = u_ref[...]
    pe_sc[0:hp, :] = ph_ref[...] * keep
    pe_sc[hp:, :] = pc_ref[...]
    y_conv = (ue_sc[hu - 2:hu - 2 + tm, :] * cw_ref[0:1, :]
              + ue_sc[hu - 1:hu - 1 + tm, :] * cw_ref[1:2, :]
              + u_ref[...] * cw_ref[2:3, :])
    b_out = bb_ref[...] * y_conv
    p = pc_ref[...]

    def back(jj):
        return pe_sc[hp - jj:hp - jj + tm, :]

    s2 = p + back(1)
    s4 = s2 + back(2) + back(3)
    s8 = s4 + back(4) + back(5) + back(6) + back(7)
    s16 = s8
    for jj in range(8, 16):
        s16 = s16 + back(jj)
    pos = (i % tiles_per_seq) * tm + lax.broadcasted_iota(jnp.int32, (tm, 1), 0)
    posf = (pos + 1).astype(F32)
    means = _pool_select(s2 / jnp.minimum(2.0, posf), s4 / jnp.minimum(4.0, posf),
                         s8 / jnp.minimum(8.0, posf), s16 / jnp.minimum(16.0, posf))
    d = means - p
    o_ref[...] = _merge_tail(a_ref[...], b_out, d, gt_ref, x_ref[...], mod_ref,
                             wa_ref, wb_ref, wc_ref, wo_ref, pw_ref, ps_ref, gp_ref)


def _mix_sample_kernel(a_ref, bb_ref, u_ref, pc_ref, ch_ref, ph_ref, gt_ref, x_ref, mod_ref,
                       cw_ref, wa_ref, wb_ref, wc_ref, wo_ref, pw_ref, ps_ref, gp_ref, o_ref):
    u = u_ref[...]
    y_conv = ch_ref[:, 0, :] * cw_ref[0:1, :] + ch_ref[:, 1, :] * cw_ref[1:2, :] + u * cw_ref[2:3, :]
    b_out = bb_ref[...] * y_conv
    p = pc_ref[...]
    sums = []
    acc = p
    nxt = POOL_HIST - 1
    for w in POOL_WINDOWS:
        while POOL_HIST - nxt < w:
            acc = acc + ph_ref[:, nxt, :]
            nxt -= 1
        sums.append(acc / float(w))
    d = _pool_select(*sums) - p
    o_ref[...] = _merge_tail(a_ref[...], b_out, d, gt_ref, x_ref[...], mod_ref,
                             wa_ref, wb_ref, wc_ref, wo_ref, pw_ref, ps_ref, gp_ref)


def _const_specs(shapes):
    return [pl.BlockSpec(s, lambda i, n=len(s): (0,) * n) for s in shapes]


def _mix_prompt(a, bb, u, pc, gt, x, mod, wts, tm, seq):
    t = x.shape[0]
    tiles_per_seq = seq // tm
    hu, hp = SUBLANES, 2 * SUBLANES
    row = lambda w: pl.BlockSpec((tm, w), lambda i: (i, 0))
    in_specs = [row(WIDTH_A), row(WIDTH_B), row(WIDTH_B), row(WIDTH_C),
                pl.BlockSpec((hu, WIDTH_B), lambda i: (jnp.maximum(i * (tm // hu) - 1, 0), 0)),
                pl.BlockSpec((hp, WIDTH_C), lambda i: (jnp.maximum(i * (tm // hp) - 1, 0), 0)),
                row(N_BRANCHES * D_MODEL), row(D_MODEL), _mod_spec(tm, seq)]
    in_specs += _const_specs([w.shape for w in wts])
    return pl.pallas_call(
        functools.partial(_mix_prompt_kernel, tm=tm, tiles_per_seq=tiles_per_seq),
        grid=(t // tm,),
        in_specs=in_specs,
        out_specs=row(D_MODEL),
        out_shape=jax.ShapeDtypeStruct((t, D_MODEL), F32),
        scratch_shapes=[pltpu.VMEM((hu + tm, WIDTH_B), F32), pltpu.VMEM((hp + tm, WIDTH_C), F32)],
        compiler_params=_cparams(("arbitrary",)),
        name="mix_prompt",
    )(a, bb, u, pc, u, pc, gt, x, mod, *wts)


def _mix_sample(a, bb, u, pc, conv_hist, pool_hist, gt, x, mod, wts):
    t = x.shape[0]
    shapes = [(t, WIDTH_A), (t, WIDTH_B), (t, WIDTH_B), (t, WIDTH_C), conv_hist.shape, pool_hist.shape,
              (t, N_BRANCHES * D_MODEL), (t, D_MODEL), (t, N_MOD, D_MODEL)] + [w.shape for w in wts]
    return pl.pallas_call(
        _mix_sample_kernel,
        grid=(1,),
        in_specs=_const_specs(shapes),
        out_specs=pl.BlockSpec((t, D_MODEL), lambda i: (0, 0)),
        out_shape=jax.ShapeDtypeStruct((t, D_MODEL), F32),
        compiler_params=_cparams(("arbitrary",)),
        name="mix_sample",
    )(a, bb, u, pc, conv_hist, pool_hist, gt, x, mod, *wts)


def _first_index(mask, idx, big, axis):
    return jnp.min(jnp.where(mask, idx, big), axis=axis, keepdims=True)


def _router_kernel(x_ref, mod_ref, g_ref, wr_ref, rb_ref, upper_ref,
                   h_ref, eid_ref, wt_ref, rank_ref, comb_ref, cnt_ref, carry_sc, *, tm):
    i = pl.program_id(0)

    @pl.when(i == 0)
    def _():
        carry_sc[...] = jnp.zeros(carry_sc.shape, F32)

    h = _rms(x_ref[...], g_ref[...]) * (1.0 + mod_ref[:, 4, :]) + mod_ref[:, 3, :]
    h_ref[...] = h
    logits = lax.dot_general(wr_ref[...], h.astype(BF16), (((1,), (1,)), ((), ())),
                             preferred_element_type=F32)
    s = jax.nn.sigmoid(logits)
    sel = s + rb_ref[...]
    sg = sel.reshape(N_EXPERT_GROUPS, GROUP_SIZE, tm)
    in_idx = lax.broadcasted_iota(jnp.int32, sg.shape, 1)
    top1 = jnp.max(sg, axis=1, keepdims=True)
    f1 = _first_index(sg == top1, in_idx, GROUP_SIZE, 1)
    top2 = jnp.max(jnp.where(in_idx == f1, -jnp.inf, sg), axis=1, keepdims=True)
    gscore = top1 + top2
    g_idx = lax.broadcasted_iota(jnp.int32, gscore.shape, 0)
    gsel = jnp.zeros(gscore.shape, F32)
    for _ in range(TOPK_GROUPS):
        mx = jnp.max(gscore, axis=0, keepdims=True)
        hit = g_idx == _first_index(gscore == mx, g_idx, N_EXPERT_GROUPS, 0)
        gsel = jnp.where(hit, 1.0, gsel)
        gscore = jnp.where(hit, -jnp.inf, gscore)
    masked = jnp.where(gsel > 0.5, sg, -jnp.inf).reshape(N_EXPERTS, tm)
    e_idx = lax.broadcasted_iota(jnp.int32, masked.shape, 0)
    hits, ids, wts = [], [], []
    for _ in range(TOP_K):
        mx = jnp.max(masked, axis=0, keepdims=True)
        fe = _first_index(masked == mx, e_idx, N_EXPERTS, 0)
        hit = e_idx == fe
        hits.append(hit)
        ids.append(fe)
        wts.append(jnp.sum(jnp.where(hit, s, 0.0), axis=0, keepdims=True))
        masked = jnp.where(hit, -jnp.inf, masked)
    wsum = wts[0]
    for w in wts[1:]:
        wsum = wsum + w
    wts = [w / wsum * ROUTED_SCALE for w in wts]
    onehot = jnp.zeros((N_EXPERTS, tm), F32)
    comb = jnp.zeros((N_EXPERTS, tm), F32)
    for hit, w in zip(hits, wts):
        onehot = onehot + jnp.where(hit, 1.0, 0.0)
        comb = comb + jnp.where(hit, w, 0.0)
    before = jnp.dot(onehot.astype(BF16), upper_ref[...], preferred_element_type=F32) + carry_sc[...]
    ranks = [jnp.sum(jnp.where(hit, before, 0.0), axis=0, keepdims=True) for hit in hits]
    carry_new = carry_sc[...] + jnp.sum(onehot, axis=1, keepdims=True)
    carry_sc[...] = carry_new
    pad = jnp.zeros((SUBLANES - TOP_K, tm), F32)
    eid_ref[...] = jnp.concatenate(ids + [pad.astype(jnp.int32)], axis=0)
    wt_ref[...] = jnp.concatenate(wts + [pad], axis=0)
    rank_ref[...] = jnp.concatenate(ranks + [pad], axis=0).astype(jnp.int32)
    comb_ref[...] = comb
    cnt_ref[...] = carry_new.astype(jnp.int32)


def _router(x, mod, g, wr_t, rb, upper, tm, rows_per_mod):
    t = x.shape[0]
    col = lambda rows: pl.BlockSpec((rows, tm), lambda i: (0, i))
    return pl.pallas_call(
        functools.partial(_router_kernel, tm=tm),
        grid=(t // tm,),
        in_specs=[pl.BlockSpec((tm, D_MODEL), lambda i: (i, 0)),
                  _mod_spec(tm, rows_per_mod),
                  pl.BlockSpec((1, D_MODEL), lambda i: (0, 0)),
                  pl.BlockSpec((N_EXPERTS, D_MODEL), lambda i: (0, 0)),
                  pl.BlockSpec((N_EXPERTS, 1), lambda i: (0, 0)),
                  pl.BlockSpec((tm, tm), lambda i: (0, 0))],
        out_specs=[pl.BlockSpec((tm, D_MODEL), lambda i: (i, 0)),
                   col(SUBLANES), col(SUBLANES), col(SUBLANES), col(N_EXPERTS),
                   pl.BlockSpec((N_EXPERTS, 1), lambda i: (0, 0))],
        out_shape=[jax.ShapeDtypeStruct((t, D_MODEL), F32),
                   jax.ShapeDtypeStruct((SUBLANES, t), jnp.int32),
                   jax.ShapeDtypeStruct((SUBLANES, t), F32),
                   jax.ShapeDtypeStruct((SUBLANES, t), jnp.int32),
                   jax.ShapeDtypeStruct((N_EXPERTS, t), F32),
                   jax.ShapeDtypeStruct((N_EXPERTS, 1), jnp.int32)],
        scratch_shapes=[pltpu.VMEM((N_EXPERTS, 1), F32)],
        compiler_params=_cparams(("arbitrary",)),
        name="router",
    )(x, mod, g, wr_t, rb, upper)


def _row_copy_wait(src_tile, dst_tile, sem):
    pltpu.make_async_copy(src_tile, dst_tile, sem).wait()


def _expert_kernel(te_ref, nv_ref, x_ref, wg_ref, wu_ref, wd_ref, y_ref, wgu_sc, wd_sc):
    t = pl.program_id(0)
    prev = te_ref[jnp.maximum(t - 1, 0)]
    fresh = jnp.logical_or(t == 0, te_ref[t] != prev)

    @pl.when(fresh)
    def _():
        wgu_sc[:, 0:D_EXPERT] = wg_ref[0, 0].astype(BF16)
        wgu_sc[:, D_EXPERT:] = wu_ref[0, 0].astype(BF16)
        wd_sc[...] = wd_ref[0, 0].astype(BF16)

    @pl.when(t < nv_ref[0])
    def _():
        gu = jnp.dot(x_ref[...].astype(BF16), wgu_sc[...], preferred_element_type=F32)
        a = _silu(gu[:, :D_EXPERT]) * gu[:, D_EXPERT:]
        y_ref[...] = jnp.dot(a.astype(BF16), wd_sc[...], preferred_element_type=F32)

    @pl.when(t >= nv_ref[0])
    def _():
        y_ref[...] = jnp.zeros(y_ref.shape, F32)


def _experts(layer, xs, tile_expert, n_valid, we_gate, we_up, we_down):
    n_tiles = xs.shape[0] // TE
    rows = lambda t, te, nv: (t, 0)
    wmap = lambda t, te, nv: (layer, te[t], 0, 0)
    grid_spec = pltpu.PrefetchScalarGridSpec(
        num_scalar_prefetch=2,
        grid=(n_tiles,),
        in_specs=[pl.BlockSpec((TE, D_MODEL), rows),
                  pl.BlockSpec((1, 1, D_MODEL, D_EXPERT), wmap),
                  pl.BlockSpec((1, 1, D_MODEL, D_EXPERT), wmap),
                  pl.BlockSpec((1, 1, D_EXPERT, D_MODEL), wmap)],
        out_specs=pl.BlockSpec((TE, D_MODEL), rows),
        scratch_shapes=[pltpu.VMEM((D_MODEL, 2 * D_EXPERT), BF16),
                        pltpu.VMEM((D_EXPERT, D_MODEL), BF16)])
    return pl.pallas_call(
        _expert_kernel,
        grid_spec=grid_spec,
        out_shape=jax.ShapeDtypeStruct((xs.shape[0], D_MODEL), F32),
        compiler_params=_cparams(("arbitrary",)),
        name="experts",
    )(tile_expert, n_valid, xs, we_gate, we_up, we_down)


def _dense_expert_kernel(h_ref, comb_ref, wg_ref, wu_ref, wd_ref, o_ref, acc_sc):
    e = pl.program_id(0)

    @pl.when(e == 0)
    def _():
        acc_sc[...] = jnp.zeros(acc_sc.shape, F32)

    hb = h_ref[...].astype(BF16)
    lane = lax.broadcasted_iota(jnp.int32, comb_ref.shape, 1)
    ce = jnp.sum(jnp.where(lane == e, comb_ref[...], 0.0), axis=1, keepdims=True)
    g = jnp.dot(hb, wg_ref[0, 0].astype(BF16), preferred_element_type=F32)
    u = jnp.dot(hb, wu_ref[0, 0].astype(BF16), preferred_element_type=F32)
    a = _silu(g) * u * ce
    acc_sc[...] += jnp.dot(a.astype(BF16), wd_ref[0, 0].astype(BF16), preferred_element_type=F32)

    @pl.when(e == pl.num_programs(0) - 1)
    def _():
        o_ref[...] = acc_sc[...]


def _dense_experts(layer, h, comb, we_gate, we_up, we_down):
    t = h.shape[0]
    wmap = lambda e: (layer, e, 0, 0)
    return pl.pallas_call(
        _dense_expert_kernel,
        grid=(N_EXPERTS,),
        in_specs=[pl.BlockSpec((t, D_MODEL), lambda e: (0, 0)),
                  pl.BlockSpec((t, N_EXPERTS), lambda e: (0, 0)),
                  pl.BlockSpec((1, 1, D_MODEL, D_EXPERT), wmap),
                  pl.BlockSpec((1, 1, D_MODEL, D_EXPERT), wmap),
                  pl.BlockSpec((1, 1, D_EXPERT, D_MODEL), wmap)],
        out_specs=pl.BlockSpec((t, D_MODEL), lambda e: (0, 0)),
        out_shape=jax.ShapeDtypeStruct((t, D_MODEL), F32),
        scratch_shapes=[pltpu.VMEM((t, D_MODEL), F32)],
        compiler_params=_cparams(("arbitrary",)),
        name="dense_experts",
    )(h, comb, we_gate, we_up, we_down)


def _ffn_tail(routed, x_ref, h_ref, mod_ref, wgu_ref, wd_ref, g_ref):
    gu = _bdot(h_ref[...], wgu_ref[...])
    a = _silu(gu[:, :D_EXPERT]) * gu[:, D_EXPERT:]
    y = routed + _bdot(a, wd_ref[...])
    return x_ref[...] + mod_ref[:, 5, :] * _rms(y, g_ref[...])


def _ffn_out_kernel(x_ref, r_ref, h_ref, mod_ref, wgu_ref, wd_ref, g_ref, o_ref):
    o_ref[...] = _ffn_tail(r_ref[...], x_ref, h_ref, mod_ref, wgu_ref, wd_ref, g_ref)


def _ffn_out(x, routed, h, mod, ws_gu, ws_d, g, tm, rows_per_mod):
    t = x.shape[0]
    row = pl.BlockSpec((tm, D_MODEL), lambda i: (i, 0))
    return pl.pallas_call(
        _ffn_out_kernel,
        grid=(t // tm,),
        in_specs=[row, row, row, _mod_spec(tm, rows_per_mod),
                  pl.BlockSpec((D_MODEL, 2 * D_EXPERT), lambda i: (0, 0)),
                  pl.BlockSpec((D_EXPERT, D_MODEL), lambda i: (0, 0)),
                  pl.BlockSpec((1, D_MODEL), lambda i: (0, 0))],
        out_specs=row,
        out_shape=jax.ShapeDtypeStruct((t, D_MODEL), F32),
        compiler_params=_cparams(("arbitrary",)),
        name="ffn_out",
    )(x, routed, h, mod, ws_gu, ws_d, g)


def _combine_kernel(dest_ref, y_ref, wt_ref, x_ref, h_ref, mod_ref, wgu_ref, wd_ref, g_ref,
                    o_ref, rows_sc, sem, *, tm):
    def rows(r0, c):
        for dr in range(DMA_UNROLL):
            r = r0 * DMA_UNROLL + dr
            for k in range(TOP_K):
                d = dest_ref[0, 0, r * TOP_K + k]
                pltpu.make_async_copy(y_ref.at[pl.ds(d, 1), :], rows_sc.at[k, pl.ds(r, 1), :], sem).start()
        return c

    lax.fori_loop(0, tm // DMA_UNROLL, rows, 0)
    for k in range(TOP_K):
        _row_copy_wait(y_ref.at[pl.ds(0, tm), :], rows_sc.at[k], sem)
    routed = rows_sc[0] * wt_ref[:, 0:1]
    for k in range(1, TOP_K):
        routed = routed + rows_sc[k] * wt_ref[:, k:k + 1]
    o_ref[...] = _ffn_tail(routed, x_ref, h_ref, mod_ref, wgu_ref, wd_ref, g_ref)


def _combine_ffn_out(x, y, dest_tiles, wt, h, mod, ws_gu, ws_d, g, tm, rows_per_mod):
    t = x.shape[0]
    row = pl.BlockSpec((tm, D_MODEL), lambda i: (i, 0))
    return pl.pallas_call(
        functools.partial(_combine_kernel, tm=tm),
        grid=(t // tm,),
        in_specs=[pl.BlockSpec((1, 1, tm * TOP_K), lambda i: (i, 0, 0), memory_space=pltpu.SMEM),
                  pl.BlockSpec(memory_space=pl.ANY),
                  pl.BlockSpec((tm, SUBLANES), lambda i: (i, 0)),
                  row, row, _mod_spec(tm, rows_per_mod),
                  pl.BlockSpec((D_MODEL, 2 * D_EXPERT), lambda i: (0, 0)),
                  pl.BlockSpec((D_EXPERT, D_MODEL), lambda i: (0, 0)),
                  pl.BlockSpec((1, D_MODEL), lambda i: (0, 0))],
        out_specs=row,
        out_shape=jax.ShapeDtypeStruct((t, D_MODEL), F32),
        scratch_shapes=[pltpu.VMEM((TOP_K, tm, D_MODEL), F32), pltpu.SemaphoreType.DMA(())],
        compiler_params=_cparams(("arbitrary",)),
        name="combine_ffn_out",
    )(dest_tiles, y, wt, x, h, mod, ws_gu, ws_d, g)


def _pack_w_in(w):
    f = jnp.pad(w[:, OFF_F:OFF_BB], ((0, 0), (0, F_PAD - N_HEADS)))
    return jnp.concatenate([w[:, OFF_Q:OFF_F], w[:, OFF_BB:OFF_PC], w[:, OFF_PC:OFF_G], w[:, OFF_G:], f],
                           axis=1).astype(BF16)


def _block_diag(pool_w):
    out = jnp.zeros((WIDTH_C, WIDTH_C), F32)
    for g in range(len(POOL_WINDOWS)):
        lo = g * GROUP_DIM_C
        out = out.at[lo:lo + GROUP_DIM_C, lo:lo + GROUP_DIM_C].set(pool_w[g])
    return out.astype(BF16)


def kernel(x_prompt, x_sample, cache_k, cache_v, cache_logf, state_conv, state_pool, page_table,
           c_prompt, c_sample, w_in, b_forget, conv_w, pool_w, pool_scale, w_br_attn, w_br_conv,
           w_br_pool, w_out, g_pre_mix, g_post_mix, g_pre_ffn, g_post_ffn, w_ada, b_ada, w_router,
           router_bias, we_gate, we_up, we_down, ws_gate, ws_up, ws_down):
    bp, seq, d = x_prompt.shape
    bd = x_sample.shape[0]
    tp = bp * seq
    n_phys = cache_k.shape[1]

    xp = x_prompt.reshape(tp, d)
    xs = x_sample.reshape(bd, d)
    c_all = jnp.concatenate([c_prompt, c_sample], axis=0)
    cache_kt = cache_k.transpose(0, 1, 3, 4, 2).reshape(DEPTH, n_phys, WIDTH_A, PAGE_SIZE)
    cache_vt = cache_v.transpose(0, 1, 3, 4, 2).reshape(DEPTH, n_phys, WIDTH_A, PAGE_SIZE)
    cache_lft = cache_logf.transpose(0, 1, 3, 2)

    pos = jnp.arange(PAGE_SIZE)
    tri = jnp.concatenate([(pos[:, None] > pos[None, :]).astype(BF16),
                           jnp.ones((PAGE_SIZE, PAGE_SIZE), BF16)], axis=1)
    tok = jnp.arange(TM)
    upper = (tok[:, None] < tok[None, :]).astype(BF16)
    tok_s = jnp.arange(bd)
    upper_s = (tok_s[:, None] < tok_s[None, :]).astype(BF16)

    n_pairs = tp * TOP_K
    n_rows_pad = n_pairs + N_EXPERTS * TE
    n_tiles = n_rows_pad // TE
    experts_iota = jnp.arange(N_EXPERTS, dtype=jnp.int32)

    outs = {name: [] for name in ("kp", "vp", "lfp", "cvp", "plp", "ks", "vs", "lfs", "cvs", "pls")}
    for l in range(DEPTH):
        mod = _modulation(c_all, w_ada[l], b_ada[l]).reshape(bp + bd, N_MOD, d)
        mod_p, mod_s = mod[:bp], mod[bp:]
        w_pack = _pack_w_in(w_in[l])
        bf_pad = jnp.pad(b_forget[l], (0, F_PAD - N_HEADS)).reshape(1, F_PAD)
        mix_w = (conv_w[l], w_br_attn[l].astype(BF16), w_br_conv[l].astype(BF16),
                 w_br_pool[l].astype(BF16), w_out[l].astype(BF16), _block_diag(pool_w[l]),
                 pool_scale[l].reshape(1, -1), g_post_mix[l].reshape(1, -1))
        wr_t = w_router[l].T.astype(BF16)
        rb = router_bias[l].reshape(N_EXPERTS, 1)
        ws_gu = jnp.concatenate([ws_gate[l], ws_up[l]], axis=1).astype(BF16)
        ws_d = ws_down[l].astype(BF16)
        g1 = g_pre_mix[l].reshape(1, -1)
        g3 = g_pre_ffn[l].reshape(1, -1)
        g4 = g_post_ffn[l].reshape(1, -1)

        q, k, v, lfpad, bb, u, pc, gt = _in_proj(xp, mod_p, g1, w_pack, bf_pad, TM, seq)
        lf = lfpad[:, :N_HEADS].reshape(bp, seq, N_HEADS)
        exq, exk = _forget_bias(lfpad, bp, seq)
        a = _prompt_attention(q, k, v, exq, exk, bp, seq)
        xp = _mix_prompt(a, bb, u, pc, gt, xp, mod_p, mix_w, TM, seq)
        h2, eid, wt, rank, _, counts = _router(xp, mod_p, g3, wr_t, rb, upper, TM, seq)
        counts = counts[:, 0]
        padded = ((counts + TE - 1) // TE) * TE
        ends = jnp.cumsum(padded)
        starts = ends - padded
        eid6, rank6 = eid[:TOP_K], rank[:TOP_K]
        start_of = jnp.sum(jnp.where(eid6[:, :, None] == experts_iota, starts, 0), axis=-1)
        dest = (start_of + rank6).T
        tile_start = jnp.arange(n_tiles, dtype=jnp.int32) * TE
        n_valid = ends[-1] // TE
        tile_expert = jnp.sum((tile_start[:, None] >= ends[None, :]).astype(jnp.int32), axis=1)
        last_expert = jnp.sum((ends[-1] - TE >= ends).astype(jnp.int32))
        tile_expert = jnp.minimum(tile_expert, last_expert)

        outs["kp"].append(k.reshape(bp, seq, N_HEADS, HEAD_DIM))
        outs["vp"].append(v.reshape(bp, seq, N_HEADS, HEAD_DIM))
        outs["lfp"].append(lf)
        outs["cvp"].append(u.reshape(bp, seq, WIDTH_B)[:, seq - (CONV_WIDTH - 1):])
        outs["plp"].append(pc.reshape(bp, seq, WIDTH_C)[:, seq - POOL_HIST:])

        q, k, v, lfpad, bb, u, pc, gt = _in_proj(xs, mod_s, g1, w_pack, bf_pad, bd, 1)
        lf = lfpad[:, :N_HEADS]
        vec = lambda z: z.reshape(bd, 1, WIDTH_A)
        a, xs_sorted = _decode_attention_dispatch(
            l, vec(q), vec(k), vec(v), lf.reshape(bd, N_HEADS, 1), cache_kt, cache_vt, cache_lft,
            page_table, tri, h2, dest, ends, n_rows_pad)
        a = a.reshape(bd, WIDTH_A)

        y = _experts(l, xs_sorted, tile_expert, n_valid.reshape(1), we_gate, we_up, we_down)
        xp = _combine_ffn_out(xp, y, dest.reshape(tp // TM, 1, TM * TOP_K), wt.T, h2, mod_p,
                              ws_gu, ws_d, g4, TM, seq)

        xs = _mix_sample(a, bb, u, pc, state_conv[l], state_pool[l], gt, xs, mod_s, mix_w)
        h2, _, _, _, comb, _ = _router(xs, mod_s, g3, wr_t, rb, upper_s, bd, 1)
        routed = _dense_experts(l, h2, comb.T, we_gate, we_up, we_down)
        xs = _ffn_out(xs, routed, h2, mod_s, ws_gu, ws_d, g4, bd, 1)

        outs["ks"].append(k.reshape(bd, 1, N_HEADS, HEAD_DIM))
        outs["vs"].append(v.reshape(bd, 1, N_HEADS, HEAD_DIM))
        outs["lfs"].append(lf.reshape(bd, 1, N_HEADS))
        outs["cvs"].append(jnp.concatenate([state_conv[l][:, 1:], u[:, None, :]], axis=1))
        outs["pls"].append(jnp.concatenate([state_pool[l][:, 1:], pc[:, None, :]], axis=1))

    st = {name: jnp.stack(vals) for name, vals in outs.items()}
    return (xp.reshape(bp, seq, d), xs.reshape(bd, 1, d), st["kp"], st["vp"], st["lfp"], st["cvp"],
            st["plp"], st["ks"], st["vs"], st["lfs"], st["cvs"], st["pls"])
```

```python
import functools

import jax
import jax.numpy as jnp
import numpy as np
from jax import lax
from jax.experimental import pallas as pl
from jax.experimental.pallas import tpu as pltpu

F32 = jnp.float32
BF16 = jnp.bfloat16

D_MODEL = 1024
DEPTH = 2
PAGE_SIZE = 128
N_HEADS = 8
HEAD_DIM = 64
WIDTH_A = N_HEADS * HEAD_DIM
WIDTH_B = 256
CONV_WIDTH = 3
POOL_WINDOWS = (2, 4, 8, 16)
WIDTH_C = 256
GROUP_DIM_C = WIDTH_C // len(POOL_WINDOWS)
POOL_HIST = max(POOL_WINDOWS) - 1
N_BRANCHES = 3
OFF_Q = 0
OFF_K = OFF_Q + WIDTH_A
OFF_V = OFF_K + WIDTH_A
OFF_F = OFF_V + WIDTH_A
OFF_BB = OFF_F + N_HEADS
OFF_CB = OFF_BB + WIDTH_B
OFF_HB = OFF_CB + WIDTH_B
OFF_PC = OFF_HB + WIDTH_B
OFF_G = OFF_PC + WIDTH_C
N_IN = OFF_G + N_BRANCHES * D_MODEL
N_EXPERTS = 64
TOP_K = 6
N_EXPERT_GROUPS = 8
GROUP_SIZE = N_EXPERTS // N_EXPERT_GROUPS
TOPK_GROUPS = 4
D_EXPERT = 256
ROUTED_SCALE = 2.5
N_MOD = 6
RMS_EPS = 1e-6

LANES = 128
SUBLANES = 8
F_PAD = LANES
NEG_BIG = -1e30
VMEM_LIMIT = 56 * 1024 * 1024

C_Q, C_K, C_V = 0, 512, 1024
C_BCH = 1536
C_PC = 2304
C_G = 2560
C_F = C_G + N_BRANCHES * D_MODEL
N_PACK = C_F + F_PAD

TM = 256
TE = 512
TQ = 512
NP_STEP = 8
DMA_UNROLL = 4


def _cparams(sem):
    return pltpu.CompilerParams(dimension_semantics=sem, vmem_limit_bytes=VMEM_LIMIT)


def _rms(x, g):
    ms = jnp.mean(x * x, axis=-1, keepdims=True)
    return x * lax.rsqrt(ms + RMS_EPS) * g


def _silu(x):
    return x * jax.nn.sigmoid(x)


def _log_sigmoid(x):
    return jnp.minimum(x, 0.0) - jnp.log1p(jnp.exp(-jnp.abs(x)))


def _bdot(a, b):
    return jnp.dot(a.astype(BF16), b.astype(BF16), preferred_element_type=F32)


def _mod_spec(tm, rows_per_mod):
    if rows_per_mod > 1:
        return pl.BlockSpec((1, N_MOD, D_MODEL), lambda i: (i // (rows_per_mod // tm), 0, 0))
    return pl.BlockSpec((tm, N_MOD, D_MODEL), lambda i: (i, 0, 0))


def _mod_kernel(c_ref, w_ref, b_ref, o_ref):
    o_ref[...] = _bdot(_silu(c_ref[...]), w_ref[...]) + b_ref[...]


def _modulation(c_all, w_ada, b_ada):
    n = c_all.shape[0]
    tn = 1536
    return pl.pallas_call(
        _mod_kernel,
        grid=(N_MOD * D_MODEL // tn,),
        in_specs=[pl.BlockSpec((n, D_MODEL), lambda j: (0, 0)),
                  pl.BlockSpec((D_MODEL, tn), lambda j: (0, j)),
                  pl.BlockSpec((1, tn), lambda j: (0, j))],
        out_specs=pl.BlockSpec((n, tn), lambda j: (0, j)),
        out_shape=jax.ShapeDtypeStruct((n, N_MOD * D_MODEL), F32),
        compiler_params=_cparams(("arbitrary",)),
        name="modulation",
    )(c_all, w_ada, b_ada.reshape(1, -1))


def _in_proj_kernel(x_ref, mod_ref, g_ref, w_ref, bf_ref,
                    q_ref, k_ref, v_ref, lf_ref, bb_ref, u_ref, pc_ref, gt_ref):
    h = _rms(x_ref[...], g_ref[...]) * (1.0 + mod_ref[:, 1, :]) + mod_ref[:, 0, :]
    hb = h.astype(BF16)

    def mm(lo, hi):
        return jnp.dot(hb, w_ref[:, lo:hi], preferred_element_type=F32)

    q_ref[...] = mm(C_Q, C_K)
    k_ref[...] = mm(C_K, C_V)
    v_ref[...] = mm(C_V, C_BCH)
    z = mm(C_BCH, C_PC)
    bb_ref[...] = z[:, :WIDTH_B]
    u_ref[...] = z[:, WIDTH_B:2 * WIDTH_B] * z[:, 2 * WIDTH_B:]
    pc_ref[...] = mm(C_PC, C_G)
    step = 512
    for j in range(N_BRANCHES * D_MODEL // step):
        gt_ref[:, j * step:(j + 1) * step] = jax.nn.sigmoid(mm(C_G + j * step, C_G + (j + 1) * step))
    lf_ref[...] = _log_sigmoid(mm(C_F, N_PACK) + bf_ref[...])


def _in_proj(x, mod, g, w_pack, bf_pad, tm, rows_per_mod):
    t = x.shape[0]
    widths = (WIDTH_A, WIDTH_A, WIDTH_A, F_PAD, WIDTH_B, WIDTH_B, WIDTH_C, N_BRANCHES * D_MODEL)
    return pl.pallas_call(
        _in_proj_kernel,
        grid=(t // tm,),
        in_specs=[pl.BlockSpec((tm, D_MODEL), lambda i: (i, 0)),
                  _mod_spec(tm, rows_per_mod),
                  pl.BlockSpec((1, D_MODEL), lambda i: (0, 0)),
                  pl.BlockSpec((D_MODEL, N_PACK), lambda i: (0, 0)),
                  pl.BlockSpec((1, F_PAD), lambda i: (0, 0))],
        out_specs=[pl.BlockSpec((tm, w), lambda i: (i, 0)) for w in widths],
        out_shape=[jax.ShapeDtypeStruct((t, w), F32) for w in widths],
        compiler_params=_cparams(("arbitrary",)),
        name="in_proj",
    )(x, mod, g, w_pack, bf_pad)


LOG2E = 1.4426950408889634
N_PIECES = 3
CUM_BLOCK = 256


def _split3(x):
    hi = x.astype(BF16).astype(F32)
    r = x - hi
    mid = r.astype(BF16).astype(F32)
    lo = (r - mid).astype(BF16).astype(F32)
    return hi, mid, lo


def _forget_bias_kernel(lf_ref, tril_ref, pq_ref, pk_ref, oq_ref, ok_ref, exq_ref, exk_ref):
    tril = tril_ref[...]
    carry = jnp.zeros((1, LANES), F32)
    for blk in range(lf_ref.shape[0] // CUM_BLOCK):
        rows = slice(blk * CUM_BLOCK, (blk + 1) * CUM_BLOCK)
        c = carry
        for piece in _split3(lf_ref[rows, :]):
            c = c + jnp.dot(tril, piece.astype(BF16), preferred_element_type=F32)
        carry = c[CUM_BLOCK - 1:CUM_BLOCK, :]
        pieces = jnp.concatenate([p.astype(BF16) for p in _split3(c * LOG2E)], axis=1)
        for hp in range(N_HEADS // 2):
            exq_ref[hp, rows, :] = jnp.dot(pieces, pq_ref[hp], preferred_element_type=F32) + oq_ref[...]
            exk_ref[hp, rows, :] = jnp.dot(pieces, pk_ref[hp], preferred_element_type=F32) + ok_ref[...]


def _bias_placement():
    pq =np.zeros((N_HEADS // 2, N_PIECES * LANES, LANES), np.float32)
    pk = np.zeros_like(pq)
    oq = np.zeros((1, LANES), np.float32)
    ok = np.zeros((1, LANES), np.float32)
    for hp in range(N_HEADS // 2):
        for side in range(2):
            spare = HEAD_DIM * (1 - side)
            for piece in range(N_PIECES):
                src = piece * LANES + 2 * hp + side
                pq[hp, src, spare + piece] = 1.0
                pk[hp, src, spare + N_PIECES + piece] = -1.0
    for side in range(2):
        spare = HEAD_DIM * (1 - side)
        oq[0, spare + N_PIECES:spare + 2 * N_PIECES] = 1.0
        ok[0, spare:spare + N_PIECES] = 1.0
    return (jnp.asarray(pq, BF16), jnp.asarray(pk, BF16), jnp.asarray(oq), jnp.asarray(ok))


def _forget_bias(lfpad, batch, seq):
    pos = jnp.arange(CUM_BLOCK)
    tril = (pos[:, None] >= pos[None, :]).astype(BF16)
    pq, pk, oq, ok = _bias_placement()
    n_pairs = N_HEADS // 2
    const = lambda shape: pl.BlockSpec(shape, lambda b, n=len(shape): (0,) * n)
    out_spec = pl.BlockSpec((n_pairs, seq, LANES), lambda b: (0, b, 0))
    out_shape = jax.ShapeDtypeStruct((n_pairs, batch * seq, LANES), F32)
    return pl.pallas_call(
        _forget_bias_kernel,
        grid=(batch,),
        in_specs=[pl.BlockSpec((seq, LANES), lambda b: (b, 0)),
                  const(tril.shape), const(pq.shape), const(pk.shape), const(oq.shape), const(ok.shape)],
        out_specs=[out_spec, out_spec],
        out_shape=[out_shape, out_shape],
        compiler_params=_cparams(("arbitrary",)),
        name="forget_bias",
    )(lfpad, tril, pq, pk, oq, ok)


def _attn_kernel(qi_ref, kj_ref, q_ref, k_ref, v_ref, exq_ref, exk_ref, o_ref,
                 qa_sc, m_sc, l_sc, acc_sc, *, tq, tk):
    qi = qi_ref[pl.program_id(2)]
    kj = kj_ref[pl.program_id(2)]
    lane = lax.broadcasted_iota(jnp.int32, (1, LANES), 1)
    first = lane < HEAD_DIM

    @pl.when(kj == 0)
    def _():
        qs = q_ref[...] * (HEAD_DIM ** -0.5 * LOG2E)
        ex = exq_ref[0]
        qa_sc[0] = jnp.where(first, qs, ex).astype(BF16)
        qa_sc[1] = jnp.where(first, ex, qs).astype(BF16)
        m_sc[...] = jnp.full(m_sc.shape, NEG_BIG, F32)
        l_sc[...] = jnp.zeros(l_sc.shape, F32)
        acc_sc[...] = jnp.zeros(acc_sc.shape, F32)

    def block(masked):
        kx = k_ref[...]
        ex = exk_ref[0]
        vb = v_ref[...].astype(BF16)
        if masked:
            key_pos = kj * tk + lax.broadcasted_iota(jnp.int32, (tk, 1), 0)
            q_pos = qi * tq + lax.broadcasted_iota(jnp.int32, (1, tq), 1)
            causal = key_pos <= q_pos
        for h in range(2):
            ka = (jnp.where(first, kx, ex) if h == 0 else jnp.where(first, ex, kx)).astype(BF16)
            s = lax.dot_general(ka, qa_sc[h], (((1,), (1,)), ((), ())),
                                preferred_element_type=F32)
            if masked:
                s = jnp.where(causal, s, NEG_BIG)
            m_prev = m_sc[h]
            m_new = jnp.maximum(m_prev, jnp.max(s, axis=0, keepdims=True))
            alpha = jnp.exp2(m_prev - m_new)
            p = jnp.exp2(s - m_new)
            l_sc[h] = alpha * l_sc[h] + jnp.sum(p, axis=0, keepdims=True)
            pv = lax.dot_general(vb, p.astype(BF16), (((0,), (0,)), ((), ())),
                                 preferred_element_type=F32)
            acc_sc[h] = acc_sc[h] * alpha + pv
            m_sc[h] = m_new

    @pl.when(kj < qi)
    def _():
        block(False)

    @pl.when(kj == qi)
    def _():
        block(True)
        row = lax.broadcasted_iota(jnp.int32, (LANES, 1), 0)
        o_t = jnp.where(row < HEAD_DIM, acc_sc[0] / l_sc[0], acc_sc[1] / l_sc[1])
        o_ref[...] = o_t.T


def _prompt_attention(q, k, v, exq, exk, batch, seq):
    tq = tk = TQ
    nq = seq // tq
    pairs = [(i, j) for i in range(nq) for j in range(i + 1)]
    qi_tab = jnp.asarray([p[0] for p in pairs], jnp.int32)
    kj_tab = jnp.asarray([p[1] for p in pairs], jnp.int32)

    q_map = lambda b, hp, t, qi, kj: (b * nq + qi[t], hp)
    kv_map = lambda b, hp, t, qi, kj: (b * nq + kj[t], hp)
    grid_spec = pltpu.PrefetchScalarGridSpec(
        num_scalar_prefetch=2,
        grid=(batch, N_HEADS // 2, len(pairs)),
        in_specs=[pl.BlockSpec((tq, LANES), q_map),
                  pl.BlockSpec((tk, LANES), kv_map),
                  pl.BlockSpec((tk, LANES), kv_map),
                  pl.BlockSpec((1, tq, LANES), lambda b, hp, t, qi, kj: (hp, b * nq + qi[t], 0)),
                  pl.BlockSpec((1, tk, LANES), lambda b, hp, t, qi, kj: (hp, b * nq + kj[t], 0))],
        out_specs=pl.BlockSpec((tq, LANES), q_map),
        scratch_shapes=[pltpu.VMEM((2, tq, LANES), BF16),
                        pltpu.VMEM((2, 1, tq), F32), pltpu.VMEM((2, 1, tq), F32),
                        pltpu.VMEM((2, LANES, tq), F32)])
    return pl.pallas_call(
        functools.partial(_attn_kernel, tq=tq, tk=tk),
        grid_spec=grid_spec,
        out_shape=jax.ShapeDtypeStruct((batch * seq, WIDTH_A), F32),
        compiler_params=_cparams(("arbitrary",) * 3),
        name="prompt_attention",
    )(qi_tab, kj_tab, q, k, v, exq, exk)


def _zero_fill_padding(ends_ref, xs_ref, zero_sc, sem):
    def zero_tile(start):
        return pltpu.make_async_copy(zero_sc, xs_ref.at[pl.ds(pl.multiple_of(start, TE), TE), :], sem)

    zero_sc[...] = jnp.zeros(zero_sc.shape, F32)
    n_tiles = xs_ref.shape[0] // TE
    n_used = ends_ref[N_EXPERTS - 1] // TE
    for phase in ("start", "wait"):
        prev = 0
        for e in range(N_EXPERTS):
            end = ends_ref[e]

            @pl.when(end > prev)
            def _():
                getattr(zero_tile(end - TE), phase)()
            prev = end

        def unused(t, c):
            getattr(zero_tile(t * TE), phase)()
            return c

        lax.fori_loop(n_used, n_tiles, unused, 0)


def _decode_kernel(pt_ref, q_ref, kn_ref, vn_ref, cq_ref, tri_ref, *rest, n_pages):
    k_refs = rest[:n_pages]
    v_refs = rest[n_pages:2 * n_pages]
    lf_refs = rest[2 * n_pages:3 * n_pages]
    o_ref = rest[3 * n_pages]
    m_sc, l_sc, acc_sc, carry_sc = rest[3 * n_pages + 1:]
    j = pl.program_id(1)
    nj = pl.num_programs(1)
    head_of_lane = lax.broadcasted_iota(jnp.int32, (N_HEADS, WIDTH_A), 1) // HEAD_DIM
    own = head_of_lane == lax.broadcasted_iota(jnp.int32, (N_HEADS, WIDTH_A), 0)

    @pl.when(j == 0)
    def _():
        m_sc[...] = jnp.full(m_sc.shape, NEG_BIG, F32)
        l_sc[...] = jnp.zeros(l_sc.shape, F32)
        acc_sc[...] = jnp.zeros(acc_sc.shape, F32)
        carry_sc[...] = jnp.zeros(carry_sc.shape, F32)

    qbd_f = jnp.where(own, q_ref[0] * (HEAD_DIM ** -0.5), 0.0)
    qbd = qbd_f.astype(BF16)
    cq = cq_ref[0]

    lf = jnp.concatenate([r[0, 0] for r in lf_refs], axis=0)
    hi = lf.astype(BF16)
    r1 = lf - hi.astype(F32)
    mid = r1.astype(BF16)
    lo = (r1 - mid.astype(F32)).astype(BF16)
    tri = tri_ref[...]
    sums = (jnp.dot(hi, tri, preferred_element_type=F32)
            + jnp.dot(mid, tri, preferred_element_type=F32)
            + jnp.dot(lo, tri, preferred_element_type=F32))
    later = sums[:, :PAGE_SIZE]
    total = sums[:, PAGE_SIZE:]

    carry = carry_sc[...]
    bias = [None] * n_pages
    for i in reversed(range(n_pages)):
        rows = slice(i * N_HEADS, (i + 1) * N_HEADS)
        bias[i] = later[rows] + carry + cq
        carry = carry + total[rows]
    carry_sc[...] = carry

    def pair(refs, g):
        return jnp.concatenate([refs[2 * g][0, 0], refs[2 * g + 1][0, 0]], axis=1).astype(BF16)

    n_pairs = n_pages // 2
    scores = [jnp.dot(qbd, pair(k_refs, g), preferred_element_type=F32)
              + jnp.concatenate([bias[2 * g], bias[2 * g + 1]], axis=1) for g in range(n_pairs)]
    s_max = scores[0]
    for s in scores[1:]:
        s_max = jnp.maximum(s_max, s)
    m_prev = m_sc[...]
    m_new = jnp.maximum(m_prev, jnp.max(s_max, axis=-1, keepdims=True))
    alpha = jnp.exp(m_prev - m_new)
    acc = acc_sc[...] * alpha
    p_sum = jnp.zeros((N_HEADS, 2 * PAGE_SIZE), F32)
    for g in range(n_pairs):
        p = jnp.exp(scores[g] - m_new)
        p_sum = p_sum + p
        acc = acc + lax.dot_general(p.astype(BF16), pair(v_refs, g), (((1,), (1,)), ((), ())),
                                    preferred_element_type=F32)
    l_new = l_sc[...] * alpha + jnp.sum(p_sum, axis=-1, keepdims=True)
    m_sc[...] = m_new
    l_sc[...] = l_new
    acc_sc[...] = acc

    @pl.when(j == nj - 1)
    def _():
        s_new = jnp.sum(qbd_f * kn_ref[0], axis=-1, keepdims=True)
        m_f = jnp.maximum(m_new, s_new)
        a1 = jnp.exp(m_new - m_f)
        a2 = jnp.exp(s_new - m_f)
        o = (acc * a1 + a2 * vn_ref[0]) / (l_new * a1 + a2)
        o_ref[0] = jnp.sum(jnp.where(own, o, 0.0), axis=0, keepdims=True)


def _decode_attention(layer, q, k_new, v_new, cq, cache_kt, cache_vt, cache_lft, page_table, tri):
    nb, n_pages_total = page_table.shape
    npg = NP_STEP
    nj = n_pages_total // npg

    def page_map(i):
        def f(b, j, pt):
            return (layer, pt[b, (nj - 1 - j) * npg + i], 0, 0)
        return f

    row = lambda b, j, pt: (b, 0, 0)
    vec = pl.BlockSpec((1, 1, WIDTH_A), row)
    in_specs = [vec, vec, vec, pl.BlockSpec((1, N_HEADS, 1), row),
                pl.BlockSpec((PAGE_SIZE, 2 * PAGE_SIZE), lambda b, j, pt: (0, 0))]
    in_specs += [pl.BlockSpec((1, 1, WIDTH_A, PAGE_SIZE), page_map(i)) for i in range(npg)]
    in_specs += [pl.BlockSpec((1, 1, WIDTH_A, PAGE_SIZE), page_map(i)) for i in range(npg)]
    in_specs += [pl.BlockSpec((1, 1, N_HEADS, PAGE_SIZE), page_map(i)) for i in range(npg)]
    grid_spec = pltpu.PrefetchScalarGridSpec(
        num_scalar_prefetch=1,
        grid=(nb, nj),
        in_specs=in_specs,
        out_specs=vec,
        scratch_shapes=[pltpu.VMEM((N_HEADS, 1), F32),
                        pltpu.VMEM((N_HEADS, 1), F32),
                        pltpu.VMEM((N_HEADS, WIDTH_A), F32),
                        pltpu.VMEM((N_HEADS, PAGE_SIZE), F32)])
    return pl.pallas_call(
        functools.partial(_decode_kernel, n_pages=npg),
        grid_spec=grid_spec,
        out_shape=jax.ShapeDtypeStruct((nb, 1, WIDTH_A), F32),
        compiler_params=_cparams(("arbitrary", "arbitrary")),
        name="decode_attention",
    )(page_table, q, k_new, v_new, cq, tri,
      *([cache_kt] * npg), *([cache_vt] * npg), *([cache_lft] * npg))


def _pool_select(s2, s4, s8, s16):
    lane = lax.broadcasted_iota(jnp.int32, (1, WIDTH_C), 1)
    g = lane // GROUP_DIM_C
    return jnp.where(g == 0, s2, jnp.where(g == 1, s4, jnp.where(g == 2, s8, s16)))


def _merge_tail(a, b_out, d, gt_ref, x, mod_ref, wa_ref, wb_ref, wc_ref, wo_ref, pw_ref, ps_ref, gp_ref):
    c_out = _bdot(d, pw_ref[...]) * ps_ref[...]
    merged = (gt_ref[:, 0:D_MODEL] * _bdot(a, wa_ref[...])
              + gt_ref[:, D_MODEL:2 * D_MODEL] * _bdot(b_out, wb_ref[...])
              + gt_ref[:, 2 * D_MODEL:] * _bdot(c_out, wc_ref[...]))
    m = _bdot(merged, wo_ref[...])
    return x + mod_ref[:, 2, :] * _rms(m, gp_ref[...])


def _mix_prompt_kernel(a_ref, bb_ref, u_ref, pc_ref, uh_ref, ph_ref, gt_ref, x_ref, mod_ref,
                       cw_ref, wa_ref, wb_ref, wc_ref, wo_ref, pw_ref, ps_ref, gp_ref,
                       o_ref, ue_sc, pe_sc, *, tm, tiles_per_seq):
    i = pl.program_id(0)
    keep = jnp.where((i % tiles_per_seq) == 0, 0.0, 1.0)
    hu, hp = SUBLANES, 2 * SUBLANES
    ue_sc[0:hu, :] = uh_ref[...] * keep
    ue_sc[hu:, :] = u_ref[...]
    pe_sc[0:hp, :] = ph_ref[...] * keep
    pe_sc[hp:, :] = pc_ref[...]
    y_conv = (ue_sc[hu - 2:hu - 2 + tm, :] * cw_ref[0:1, :]
              + ue_sc[hu - 1:hu - 1 + tm, :] * cw_ref[1:2, :]
              + u_ref[...] * cw_ref[2:3, :])
    b_out = bb_ref[...] * y_conv
    p = pc_ref[...]

    def back(jj):
        return pe_sc[hp - jj:hp - jj + tm, :]

    s2 = p + back(1)
    s4 = s2 + back(2) + back(3)
    s8 = s4 + back(4) + back(5) + back(6) + back(7)
    s16 = s8
    for jj in range(8, 16):
        s16 = s16 + back(jj)
    pos = (i % tiles_per_seq) * tm + lax.broadcasted_iota(jnp.int32, (tm, 1), 0)
    posf = (pos + 1).astype(F32)
    means = _pool_select(s2 / jnp.minimum(2.0, posf), s4 / jnp.minimum(4.0, posf),
                         s8 / jnp.minimum(8.0, posf), s16 / jnp.minimum(16.0, posf))
    d = means - p
    o_ref[...] = _merge_tail(a_ref[...], b_out, d, gt_ref, x_ref[...], mod_ref,
                             wa_ref, wb_ref, wc_ref, wo_ref, pw_ref, ps_ref, gp_ref)


def _mix_sample_kernel(a_ref, bb_ref, u_ref, pc_ref, ch_ref, ph_ref, gt_ref, x_ref, mod_ref,
                       cw_ref, wa_ref, wb_ref, wc_ref, wo_ref, pw_ref, ps_ref, gp_ref, o_ref):
    u = u_ref[...]
    y_conv = ch_ref[:, 0, :] * cw_ref[0:1, :] + ch_ref[:, 1, :] * cw_ref[1:2, :] + u * cw_ref[2:3, :]
    b_out = bb_ref[...] * y_conv
    p = pc_ref[...]
    sums = []
    acc = p
    nxt = POOL_HIST - 1
    for w in POOL_WINDOWS:
        while POOL_HIST - nxt < w:
            acc = acc + ph_ref[:, nxt, :]
            nxt -= 1
        sums.append(acc / float(w))
    d = _pool_select(*sums) - p
    o_ref[...] = _merge_tail(a_ref[...], b_out, d, gt_ref, x_ref[...], mod_ref,
                             wa_ref, wb_ref, wc_ref, wo_ref, pw_ref, ps_ref, gp_ref)


def _const_specs(shapes):
    return [pl.BlockSpec(s, lambda i, n=len(s): (0,) * n) for s in shapes]


def _mix_prompt(a, bb, u, pc, gt, x, mod, wts, tm, seq):
    t = x.shape[0]
    tiles_per_seq = seq // tm
    hu, hp = SUBLANES, 2 * SUBLANES
    row = lambda w: pl.BlockSpec((tm, w), lambda i: (i, 0))
    in_specs = [row(WIDTH_A), row(WIDTH_B), row(WIDTH_B), row(WIDTH_C),
                pl.BlockSpec((hu, WIDTH_B), lambda i: (jnp.maximum(i * (tm // hu) - 1, 0), 0)),
                pl.BlockSpec((hp, WIDTH_C), lambda i: (jnp.maximum(i * (tm // hp) - 1, 0), 0)),
                row(N_BRANCHES * D_MODEL), row(D_MODEL), _mod_spec(tm, seq)]
    in_specs += _const_specs([w.shape for w in wts])
    return pl.pallas_call(
        functools.partial(_mix_prompt_kernel, tm=tm, tiles_per_seq=tiles_per_seq),
        grid=(t // tm,),
        in_specs=in_specs,
        out_specs=row(D_MODEL),
        out_shape=jax.ShapeDtypeStruct((t, D_MODEL), F32),
        scratch_shapes=[pltpu.VMEM((hu + tm, WIDTH_B), F32), pltpu.VMEM((hp + tm, WIDTH_C), F32)],
        compiler_params=_cparams(("arbitrary",)),
        name="mix_prompt",
    )(a, bb, u, pc, u, pc, gt, x, mod, *wts)


def _mix_sample(a, bb, u, pc, conv_hist, pool_hist, gt, x, mod, wts):
    t = x.shape[0]
    shapes = [(t, WIDTH_A), (t, WIDTH_B), (t, WIDTH_B), (t, WIDTH_C), conv_hist.shape, pool_hist.shape,
              (t, N_BRANCHES * D_MODEL), (t, D_MODEL), (t, N_MOD, D_MODEL)] + [w.shape for w in wts]
    return pl.pallas_call(
        _mix_sample_kernel,
        grid=(1,),
        in_specs=_const_specs(shapes),
        out_specs=pl.BlockSpec((t, D_MODEL), lambda i: (0, 0)),
        out_shape=jax.ShapeDtypeStruct((t, D_MODEL), F32),
        compiler_params=_cparams(("arbitrary",)),
        name="mix_sample",
    )(a, bb, u, pc, conv_hist, pool_hist, gt, x, mod, *wts)


def _first_index(mask, idx, big, axis):
    return jnp.min(jnp.where(mask, idx, big), axis=axis, keepdims=True)


def _router_kernel(x_ref, mod_ref, g_ref, wr_ref, rb_ref, upper_ref,
                   h_ref, eid_ref, wt_ref, rank_ref, comb_ref, cnt_ref, carry_sc, *, tm):
    i = pl.program_id(0)

    @pl.when(i == 0)
    def _():
        carry_sc[...] = jnp.zeros(carry_sc.shape, F32)

    h = _rms(x_ref[...], g_ref[...]) * (1.0 + mod_ref[:, 4, :]) + mod_ref[:, 3, :]
    h_ref[...] = h
    logits = lax.dot_general(wr_ref[...], h.astype(BF16), (((1,), (1,)), ((), ())),
                             preferred_element_type=F32)
    s = jax.nn.sigmoid(logits)
    sel = s + rb_ref[...]
    sg = sel.reshape(N_EXPERT_GROUPS, GROUP_SIZE, tm)
    in_idx = lax.broadcasted_iota(jnp.int32, sg.shape, 1)
    top1 = jnp.max(sg, axis=1, keepdims=True)
    f1 = _first_index(sg == top1, in_idx, GROUP_SIZE, 1)
    top2 = jnp.max(jnp.where(in_idx == f1, -jnp.inf, sg), axis=1, keepdims=True)
    gscore = top1 + top2
    g_idx = lax.broadcasted_iota(jnp.int32, gscore.shape, 0)
    gsel = jnp.zeros(gscore.shape, F32)
    for _ in range(TOPK_GROUPS):
        mx = jnp.max(gscore, axis=0, keepdims=True)
        hit = g_idx == _first_index(gscore == mx, g_idx, N_EXPERT_GROUPS, 0)
        gsel = jnp.where(hit, 1.0, gsel)
        gscore = jnp.where(hit, -jnp.inf, gscore)
    masked = jnp.where(gsel > 0.5, sg, -jnp.inf).reshape(N_EXPERTS, tm)
    e_idx = lax.broadcasted_iota(jnp.int32, masked.shape, 0)
    hits, ids, wts = [], [], []
    for _ in range(TOP_K):
        mx = jnp.max(masked, axis=0, keepdims=True)
        fe = _first_index(masked == mx, e_idx, N_EXPERTS, 0)
        hit = e_idx == fe
        hits.append(hit)
        ids.append(fe)
        wts.append(jnp.sum(jnp.where(hit, s, 0.0), axis=0, keepdims=True))
        masked = jnp.where(hit, -jnp.inf, masked)
    wsum = wts[0]
    for w in wts[1:]:
        wsum = wsum + w
    wts = [w / wsum * ROUTED_SCALE for w in wts]
    onehot = jnp.zeros((N_EXPERTS, tm), F32)
    comb = jnp.zeros((N_EXPERTS, tm), F32)
    for hit, w in zip(hits, wts):
        onehot = onehot + jnp.where(hit, 1.0, 0.0)
        comb = comb + jnp.where(hit, w, 0.0)
    before = jnp.dot(onehot.astype(BF16), upper_ref[...], preferred_element_type=F32) + carry_sc[...]
    ranks = [jnp.sum(jnp.where(hit, before, 0.0), axis=0, keepdims=True) for hit in hits]
    carry_new = carry_sc[...] + jnp.sum(onehot, axis=1, keepdims=True)
    carry_sc[...] = carry_new
    pad = jnp.zeros((SUBLANES - TOP_K, tm), F32)
    eid_ref[...] = jnp.concatenate(ids + [pad.astype(jnp.int32)], axis=0)
    wt_ref[...] = jnp.concatenate(wts + [pad], axis=0)
    rank_ref[...] = jnp.concatenate(ranks + [pad], axis=0).astype(jnp.int32)
    comb_ref[...] = comb
    cnt_ref[...] = carry_new.astype(jnp.int32)


def _router(x, mod, g, wr_t, rb, upper, tm, rows_per_mod):
    t = x.shape[0]
    col = lambda rows: pl.BlockSpec((rows, tm), lambda i: (0, i))
    return pl.pallas_call(
        functools.partial(_router_kernel, tm=tm),
        grid=(t // tm,),
        in_specs=[pl.BlockSpec((tm, D_MODEL), lambda i: (i, 0)),
                  _mod_spec(tm, rows_per_mod),
                  pl.BlockSpec((1, D_MODEL), lambda i: (0, 0)),
                  pl.BlockSpec((N_EXPERTS, D_MODEL), lambda i: (0, 0)),
                  pl.BlockSpec((N_EXPERTS, 1), lambda i: (0, 0)),
                  pl.BlockSpec((tm, tm), lambda i: (0, 0))],
        out_specs=[pl.BlockSpec((tm, D_MODEL), lambda i: (i, 0)),
                   col(SUBLANES), col(SUBLANES), col(SUBLANES), col(N_EXPERTS),
                   pl.BlockSpec((N_EXPERTS, 1), lambda i: (0, 0))],
        out_shape=[jax.ShapeDtypeStruct((t, D_MODEL), F32),
                   jax.ShapeDtypeStruct((SUBLANES, t), jnp.int32),
                   jax.ShapeDtypeStruct((SUBLANES, t), F32),
                   jax.ShapeDtypeStruct((SUBLANES, t), jnp.int32),
                   jax.ShapeDtypeStruct((N_EXPERTS, t), F32),
                   jax.ShapeDtypeStruct((N_EXPERTS, 1), jnp.int32)],
        scratch_shapes=[pltpu.VMEM((N_EXPERTS, 1), F32)],
        compiler_params=_cparams(("arbitrary",)),
        name="router",
    )(x, mod, g, wr_t, rb, upper)


def _row_copy_wait(src_tile, dst_tile, sem):
    pltpu.make_async_copy(src_tile, dst_tile, sem).wait()


def _dispatch_kernel(ends_ref, dest_ref, h_ref, o_ref, zero_sc, sem, *, tm):
    @pl.when(pl.program_id(0) == 0)
    def _():
        _zero_fill_padding(ends_ref, o_ref, zero_sc, sem)

    def rows(r0, c):
        for dr in range(DMA_UNROLL):
            r = r0 * DMA_UNROLL + dr
            for k in range(TOP_K):
                d = dest_ref[0, 0, r * TOP_K + k]
                pltpu.make_async_copy(h_ref.at[pl.ds(r, 1), :], o_ref.at[pl.ds(d, 1), :], sem).start()
        return c

    lax.fori_loop(0, tm // DMA_UNROLL, rows, 0)
    for _ in range(TOP_K):
        _row_copy_wait(h_ref, o_ref.at[pl.ds(0, tm), :], sem)


def _dispatch(h, dest_tiles, ends, n_rows_pad, tm):
    t = h.shape[0]
    grid_spec = pltpu.PrefetchScalarGridSpec(
        num_scalar_prefetch=1,
        grid=(t // tm,),
        in_specs=[pl.BlockSpec((1, 1, tm * TOP_K), lambda i, ends: (i, 0, 0), memory_space=pltpu.SMEM),
                  pl.BlockSpec((tm, D_MODEL), lambda i, ends: (i, 0))],
        out_specs=pl.BlockSpec(memory_space=pl.ANY),
        scratch_shapes=[pltpu.VMEM((TE, D_MODEL), F32), pltpu.SemaphoreType.DMA(())])
    return pl.pallas_call(
        functools.partial(_dispatch_kernel, tm=tm),
        grid_spec=grid_spec,
        out_shape=jax.ShapeDtypeStruct((n_rows_pad, D_MODEL), F32),
        compiler_params=_cparams(("arbitrary",)),
        name="dispatch",
    )(ends, dest_tiles, h)


def _expert_kernel(te_ref, nv_ref, x_ref, wg_ref, wu_ref, wd_ref, y_ref, wgu_sc, wd_sc):
    t = pl.program_id(0)
    prev = te_ref[jnp.maximum(t - 1, 0)]
    fresh = jnp.logical_or(t == 0, te_ref[t] != prev)

    @pl.when(fresh)
    def _():
        wgu_sc[:, 0:D_EXPERT] = wg_ref[0, 0].astype(BF16)
        wgu_sc[:, D_EXPERT:] = wu_ref[0, 0].astype(BF16)
        wd_sc[...] = wd_ref[0, 0].astype(BF16)

    @pl.when(t < nv_ref[0])
    def _():
        gu = jnp.dot(x_ref[...].astype(BF16), wgu_sc[...], preferred_element_type=F32)
        a = _silu(gu[:, :D_EXPERT]) * gu[:, D_EXPERT:]
        y_ref[...] = jnp.dot(a.astype(BF16), wd_sc[...], preferred_element_type=F32)

    @pl.when(t >= nv_ref[0])
    def _():
        y_ref[...] = jnp.zeros(y_ref.shape, F32)


def _experts(layer, xs, tile_expert, n_valid, we_gate, we_up, we_down):
    n_tiles = xs.shape[0] // TE
    rows = lambda t, te, nv: (t, 0)
    wmap = lambda t, te, nv: (layer, te[t], 0, 0)
    grid_spec = pltpu.PrefetchScalarGridSpec(
        num_scalar_prefetch=2,
        grid=(n_tiles,),
        in_specs=[pl.BlockSpec((TE, D_MODEL), rows),
                  pl.BlockSpec((1, 1, D_MODEL, D_EXPERT), wmap),
                  pl.BlockSpec((1, 1, D_MODEL, D_EXPERT), wmap),
                  pl.BlockSpec((1, 1, D_EXPERT, D_MODEL), wmap)],
        out_specs=pl.BlockSpec((TE, D_MODEL), rows),
        scratch_shapes=[pltpu.VMEM((D_MODEL, 2 * D_EXPERT), BF16),
                        pltpu.VMEM((D_EXPERT, D_MODEL), BF16)])
    return pl.pallas_call(
        _expert_kernel,
        grid_spec=grid_spec,
        out_shape=jax.ShapeDtypeStruct((xs.shape[0], D_MODEL), F32),
        compiler_params=_cparams(("arbitrary",)),
        name="experts",
    )(tile_expert, n_valid, xs, we_gate, we_up, we_down)


def _dense_expert_kernel(h_ref, comb_ref, wg_ref, wu_ref, wd_ref, o_ref, acc_sc):
    e = pl.program_id(0)

    @pl.when(e == 0)
    def _():
        acc_sc[...] = jnp.zeros(acc_sc.shape, F32)

    hb = h_ref[...].astype(BF16)
    lane = lax.broadcasted_iota(jnp.int32, comb_ref.shape, 1)
    ce = jnp.sum(jnp.where(lane == e, comb_ref[...], 0.0), axis=1, keepdims=True)
    g = jnp.dot(hb, wg_ref[0, 0].astype(BF16), preferred_element_type=F32)
    u = jnp.dot(hb, wu_ref[0, 0].astype(BF16), preferred_element_type=F32)
    a = _silu(g) * u * ce
    acc_sc[...] += jnp.dot(a.astype(BF16), wd_ref[0, 0].astype(BF16), preferred_element_type=F32)

    @pl.when(e == pl.num_programs(0) - 1)
    def _():
        o_ref[...] = acc_sc[...]


def _dense_experts(layer, h, comb, we_gate, we_up, we_down):
    t = h.shape[0]
    wmap = lambda e: (layer, e, 0, 0)
    return pl.pallas_call(
        _dense_expert_kernel,
        grid=(N_EXPERTS,),
        in_specs=[pl.BlockSpec((t, D_MODEL), lambda e: (0, 0)),
                  pl.BlockSpec((t, N_EXPERTS), lambda e: (0, 0)),
                  pl.BlockSpec((1, 1, D_MODEL, D_EXPERT), wmap),
                  pl.BlockSpec((1, 1, D_MODEL, D_EXPERT), wmap),
                  pl.BlockSpec((1, 1, D_EXPERT, D_MODEL), wmap)],
        out_specs=pl.BlockSpec((t, D_MODEL), lambda e: (0, 0)),
        out_shape=jax.ShapeDtypeStruct((t, D_MODEL), F32),
        scratch_shapes=[pltpu.VMEM((t, D_MODEL), F32)],
        compiler_params=_cparams(("arbitrary",)),
        name="dense_experts",
    )(h, comb, we_gate, we_up, we_down)


def _ffn_tail(routed, x_ref, h_ref, mod_ref, wgu_ref, wd_ref, g_ref):
    gu = _bdot(h_ref[...], wgu_ref[...])
    a = _silu(gu[:, :D_EXPERT]) * gu[:, D_EXPERT:]
    y = routed + _bdot(a, wd_ref[...])
    return x_ref[...] + mod_ref[:, 5, :] * _rms(y, g_ref[...])


def _ffn_out_kernel(x_ref, r_ref, h_ref, mod_ref, wgu_ref, wd_ref, g_ref, o_ref):
    o_ref[...] = _ffn_tail(r_ref[...], x_ref, h_ref, mod_ref, wgu_ref, wd_ref, g_ref)


def _ffn_out(x, routed, h, mod, ws_gu, ws_d, g, tm, rows_per_mod):
    t = x.shape[0]
    row = pl.BlockSpec((tm, D_MODEL), lambda i: (i, 0))
    return pl.pallas_call(
        _ffn_out_kernel,
        grid=(t // tm,),
        in_specs=[row, row, row, _mod_spec(tm, rows_per_mod),
                  pl.BlockSpec((D_MODEL, 2 * D_EXPERT), lambda i: (0, 0)),
                  pl.BlockSpec((D_EXPERT, D_MODEL), lambda i: (0, 0)),
                  pl.BlockSpec((1, D_MODEL), lambda i: (0, 0))],
        out_specs=row,
        out_shape=jax.ShapeDtypeStruct((t, D_MODEL), F32),
        compiler_params=_cparams(("arbitrary",)),
        name="ffn_out",
    )(x, routed, h, mod, ws_gu, ws_d, g)


def _combine_kernel(dest_ref, y_ref, wt_ref, x_ref, h_ref, mod_ref, wgu_ref, wd_ref, g_ref,
                    o_ref, rows_sc, sem, *, tm):
    def rows(r0, c):
        for dr in range(DMA_UNROLL):
            r = r0 * DMA_UNROLL + dr
            for k in range(TOP_K):
                d = dest_ref[0, 0, r * TOP_K + k]
                pltpu.make_async_copy(y_ref.at[pl.ds(d, 1), :], rows_sc.at[k, pl.ds(r, 1), :], sem).start()
        return c

    lax.fori_loop(0, tm // DMA_UNROLL, rows, 0)
    for k in range(TOP_K):
        _row_copy_wait(y_ref.at[pl.ds(0, tm), :], rows_sc.at[k], sem)
    routed = rows_sc[0] * wt_ref[:, 0:1]
    for k in range(1, TOP_K):
        routed = routed + rows_sc[k] * wt_ref[:, k:k + 1]
    o_ref[...] = _ffn_tail(routed, x_ref, h_ref, mod_ref, wgu_ref, wd_ref, g_ref)


def _combine_ffn_out(x, y, dest_tiles, wt, h, mod, ws_gu, ws_d, g, tm, rows_per_mod):
    t = x.shape[0]
    row = pl.BlockSpec((tm, D_MODEL), lambda i: (i, 0))
    return pl.pallas_call(
        functools.partial(_combine_kernel, tm=tm),
        grid=(t // tm,),
        in_specs=[pl.BlockSpec((1, 1, tm * TOP_K), lambda i: (i, 0, 0), memory_space=pltpu.SMEM),
                  pl.BlockSpec(memory_space=pl.ANY),
                  pl.BlockSpec((tm, SUBLANES), lambda i: (i, 0)),
                  row, row, _mod_spec(tm, rows_per_mod),
                  pl.BlockSpec((D_MODEL, 2 * D_EXPERT), lambda i: (0, 0)),
                  pl.BlockSpec((D_EXPERT, D_MODEL), lambda i: (0, 0)),
                  pl.BlockSpec((1, D_MODEL), lambda i: (0, 0))],
        out_specs=row,
        out_shape=jax.ShapeDtypeStruct((t, D_MODEL), F32),
        scratch_shapes=[pltpu.VMEM((TOP_K, tm, D_MODEL), F32), pltpu.SemaphoreType.DMA(())],
        compiler_params=_cparams(("arbitrary",)),
        name="combine_ffn_out",
    )(dest_tiles, y, wt, x, h, mod, ws_gu, ws_d, g)


def _pack_w_in(w):
    f = jnp.pad(w[:, OFF_F:OFF_BB], ((0, 0), (0, F_PAD - N_HEADS)))
    return jnp.concatenate([w[:, OFF_Q:OFF_F], w[:, OFF_BB:OFF_PC], w[:, OFF_PC:OFF_G], w[:, OFF_G:], f],
                           axis=1).astype(BF16)


def _block_diag(pool_w):
    out = jnp.zeros((WIDTH_C, WIDTH_C), F32)
    for g in range(len(POOL_WINDOWS)):
        lo = g * GROUP_DIM_C
        out = out.at[lo:lo + GROUP_DIM_C, lo:lo + GROUP_DIM_C].set(pool_w[g])
    return out.astype(BF16)


def kernel(x_prompt, x_sample, cache_k, cache_v, cache_logf, state_conv, state_pool, page_table,
           c_prompt, c_sample, w_in, b_forget, conv_w, pool_w, pool_scale, w_br_attn, w_br_conv,
           w_br_pool, w_out, g_pre_mix, g_post_mix, g_pre_ffn, g_post_ffn, w_ada, b_ada, w_router,
           router_bias, we_gate, we_up, we_down, ws_gate, ws_up, ws_down):
    bp, seq, d = x_prompt.shape
    bd = x_sample.shape[0]
    tp = bp * seq
    n_phys = cache_k.shape[1]

    xp = x_prompt.reshape(tp, d)
    xs = x_sample.reshape(bd, d)
    c_all = jnp.concatenate([c_prompt, c_sample], axis=0)
    cache_kt = cache_k.transpose(0, 1, 3, 4, 2).reshape(DEPTH, n_phys, WIDTH_A, PAGE_SIZE)
    cache_vt = cache_v.transpose(0, 1, 3, 4, 2).reshape(DEPTH, n_phys, WIDTH_A, PAGE_SIZE)
    cache_lft = cache_logf.transpose(0, 1, 3, 2)

    pos = jnp.arange(PAGE_SIZE)
    tri = jnp.concatenate([(pos[:, None] > pos[None, :]).astype(BF16),
                           jnp.ones((PAGE_SIZE, PAGE_SIZE), BF16)], axis=1)
    tok = jnp.arange(TM)
    upper = (tok[:, None] < tok[None, :]).astype(BF16)
    tok_s = jnp.arange(bd)
    upper_s = (tok_s[:, None] < tok_s[None, :]).astype(BF16)

    n_pairs = tp * TOP_K
    n_rows_pad = n_pairs + N_EXPERTS * TE
    n_tiles = n_rows_pad // TE
    experts_iota = jnp.arange(N_EXPERTS, dtype=jnp.int32)

    outs = {name: [] for name in ("kp", "vp", "lfp", "cvp", "plp", "ks", "vs", "lfs", "cvs", "pls")}
    for l in range(DEPTH):
        mod = _modulation(c_all, w_ada[l], b_ada[l]).reshape(bp + bd, N_MOD, d)
        mod_p, mod_s = mod[:bp], mod[bp:]
        w_pack = _pack_w_in(w_in[l])
        bf_pad = jnp.pad(b_forget[l], (0, F_PAD - N_HEADS)).reshape(1, F_PAD)
        mix_w = (conv_w[l], w_br_attn[l].astype(BF16), w_br_conv[l].astype(BF16),
                 w_br_pool[l].astype(BF16), w_out[l].astype(BF16), _block_diag(pool_w[l]),
                 pool_scale[l].reshape(1, -1), g_post_mix[l].reshape(1, -1))
        wr_t = w_router[l].T.astype(BF16)
        rb = router_bias[l].reshape(N_EXPERTS, 1)
        ws_gu = jnp.concatenate([ws_gate[l], ws_up[l]], axis=1).astype(BF16)
        ws_d = ws_down[l].astype(BF16)
        g1 = g_pre_mix[l].reshape(1, -1)
        g3 = g_pre_ffn[l].reshape(1, -1)
        g4 = g_post_ffn[l].reshape(1, -1)

        q, k, v, lfpad, bb, u, pc, gt = _in_proj(xp, mod_p, g1, w_pack, bf_pad, TM, seq)
        lf = lfpad[:, :N_HEADS].reshape(bp, seq, N_HEADS)
        exq, exk = _forget_bias(lfpad, bp, seq)
        a = _prompt_attention(q, k, v, exq, exk, bp, seq)
        xp = _mix_prompt(a, bb, u, pc, gt, xp, mod_p, mix_w, TM, seq)
        h2, eid, wt, rank, _, counts = _router(xp, mod_p, g3, wr_t, rb, upper, TM, seq)
        counts = counts[:, 0]
        padded = ((counts + TE - 1) // TE) * TE
        ends = jnp.cumsum(padded)
        starts = ends - padded
        eid6, rank6 = eid[:TOP_K], rank[:TOP_K]
        start_of = jnp.sum(jnp.where(eid6[:, :, None] == experts_iota, starts, 0), axis=-1)
        dest_tiles = (start_of + rank6).T.reshape(tp // TM, 1, TM * TOP_K)
        tile_start = jnp.arange(n_tiles, dtype=jnp.int32) * TE
        n_valid = ends[-1] // TE
        tile_expert = jnp.sum((tile_start[:, None] >= ends[None, :]).astype(jnp.int32), axis=1)
        last_expert = jnp.sum((ends[-1] - TE >= ends).astype(jnp.int32))
        tile_expert = jnp.minimum(tile_expert, last_expert)
        xs_sorted = _dispatch(h2, dest_tiles, ends, n_rows_pad, TM)
        y = _experts(l, xs_sorted, tile_expert, n_valid.reshape(1), we_gate, we_up, we_down)
        xp = _combine_ffn_out(xp, y, dest_tiles, wt.T, h2, mod_p, ws_gu, ws_d, g4, TM, seq)

        outs["kp"].append(k.reshape(bp, seq, N_HEADS, HEAD_DIM))
        outs["vp"].append(v.reshape(bp, seq, N_HEADS, HEAD_DIM))
        outs["lfp"].append(lf)
        outs["cvp"].append(u.reshape(bp, seq, WIDTH_B)[:, seq - (CONV_WIDTH - 1):])
        outs["plp"].append(pc.reshape(bp, seq, WIDTH_C)[:, seq - POOL_HIST:])

        q, k, v, lfpad, bb, u, pc, gt = _in_proj(xs, mod_s, g1, w_pack, bf_pad, bd, 1)
        lf = lfpad[:, :N_HEADS]
        vec = lambda z: z.reshape(bd, 1, WIDTH_A)
        a = _decode_attention(l, vec(q), vec(k), vec(v), lf.reshape(bd, N_HEADS, 1),
                              cache_kt, cache_vt, cache_lft, page_table, tri).reshape(bd, WIDTH_A)
        xs = _mix_sample(a, bb, u, pc, state_conv[l], state_pool[l], gt, xs, mod_s, mix_w)
        h2, _, _, _, comb, _ = _router(xs, mod_s, g3, wr_t, rb, upper_s, bd, 1)
        routed = _dense_experts(l, h2, comb.T, we_gate, we_up, we_down)
        xs = _ffn_out(xs, routed, h2, mod_s, ws_gu, ws_d, g4, bd, 1)

        outs["ks"].append(k.reshape(bd, 1, N_HEADS, HEAD_DIM))
        outs["vs"].append(v.reshape(bd, 1, N_HEADS, HEAD_DIM))
        outs["lfs"].append(lf.reshape(bd, 1, N_HEADS))
        outs["cvs"].append(jnp.concatenate([state_conv[l][:, 1:], u[:, None, :]], axis=1))
        outs["pls"].append(jnp.concatenate([state_pool[l][:, 1:], pc[:, None, :]], axis=1))

    st = {name: jnp.stack(vals) for name, vals in outs.items()}
    return (xp.reshape(bp, seq, d), xs.reshape(bd, 1, d), st["kp"], st["vp"], st["lfp"], st["cvp"],
            st["plp"], st["ks"], st["vs"], st["lfs"], st["cvs"], st["pls"])
```

```python
import functools

import jax
import jax.numpy as jnp
import numpy as np
from jax import lax
from jax.experimental import pallas as pl
from jax.experimental.pallas import tpu as pltpu

F32 = jnp.float32
BF16 = jnp.bfloat16

D_MODEL = 1024
DEPTH = 2
PAGE_SIZE = 128
N_HEADS = 8
HEAD_DIM = 64
WIDTH_A = N_HEADS * HEAD_DIM
WIDTH_B = 256
CONV_WIDTH = 3
POOL_WINDOWS = (2, 4, 8, 16)
WIDTH_C = 256
GROUP_DIM_C = WIDTH_C // len(POOL_WINDOWS)
POOL_HIST = max(POOL_WINDOWS) - 1
N_BRANCHES = 3
OFF_Q = 0
OFF_K = OFF_Q + WIDTH_A
OFF_V = OFF_K + WIDTH_A
OFF_F = OFF_V + WIDTH_A
OFF_BB = OFF_F + N_HEADS
OFF_CB = OFF_BB + WIDTH_B
OFF_HB = OFF_CB + WIDTH_B
OFF_PC = OFF_HB + WIDTH_B
OFF_G = OFF_PC + WIDTH_C
N_IN = OFF_G + N_BRANCHES * D_MODEL
N_EXPERTS = 64
TOP_K = 6
N_EXPERT_GROUPS = 8
GROUP_SIZE = N_EXPERTS // N_EXPERT_GROUPS
TOPK_GROUPS = 4
D_EXPERT = 256
ROUTED_SCALE = 2.5
N_MOD = 6
RMS_EPS = 1e-6

LANES = 128
SUBLANES = 8
F_PAD = LANES
NEG_BIG = -1e30
VMEM_LIMIT = 56 * 1024 * 1024

C_Q, C_K, C_V = 0, 512, 1024
C_BCH = 1536
C_PC = 2304
C_G = 2560
C_F = C_G + N_BRANCHES * D_MODEL
N_PACK = C_F + F_PAD

TM = 256
TR = 512
TE = 512
TQ = 512
NP_STEP = 8
DMA_UNROLL = 4


def _cparams(sem):
    return pltpu.CompilerParams(dimension_semantics=sem, vmem_limit_bytes=VMEM_LIMIT)


def _rms(x, g):
    ms = jnp.mean(x * x, axis=-1, keepdims=True)
    return x * lax.rsqrt(ms + RMS_EPS) * g


def _silu(x):
    return x * jax.nn.sigmoid(x)


def _log_sigmoid(x):
    return jnp.minimum(x, 0.0) - jnp.log1p(jnp.exp(-jnp.abs(x)))


def _bdot(a, b):
    return jnp.dot(a.astype(BF16), b.astype(BF16), preferred_element_type=F32)


def _mod_spec(tm, rows_per_mod):
    if rows_per_mod > 1:
        return pl.BlockSpec((1, N_MOD, D_MODEL), lambda i: (i // (rows_per_mod // tm), 0, 0))
    return pl.BlockSpec((tm, N_MOD, D_MODEL), lambda i: (i, 0, 0))


def _mod_kernel(c_ref, w_ref, b_ref, o_ref):
    o_ref[...] = _bdot(_silu(c_ref[...]), w_ref[...]) + b_ref[...]


def _modulation(c_all, w_ada, b_ada):
    n = c_all.shape[0]
    tn = 1536
    return pl.pallas_call(
        _mod_kernel,
        grid=(N_MOD * D_MODEL // tn,),
        in_specs=[pl.BlockSpec((n, D_MODEL), lambda j: (0, 0)),
                  pl.BlockSpec((D_MODEL, tn), lambda j: (0, j)),
                  pl.BlockSpec((1, tn), lambda j: (0, j))],
        out_specs=pl.BlockSpec((n, tn), lambda j: (0, j)),
        out_shape=jax.ShapeDtypeStruct((n, N_MOD * D_MODEL), F32),
        compiler_params=_cparams(("arbitrary",)),
        name="modulation",
    )(c_all, w_ada, b_ada.reshape(1, -1))


def _in_proj_kernel(x_ref, mod_ref, g_ref, w_ref, bf_ref,
                    q_ref, k_ref, v_ref, lf_ref, bb_ref, u_ref, pc_ref, gt_ref):
    h = _rms(x_ref[...], g_ref[...]) * (1.0 + mod_ref[:, 1, :]) + mod_ref[:, 0, :]
    hb = h.astype(BF16)

    def mm(lo, hi):
        return jnp.dot(hb, w_ref[:, lo:hi], preferred_element_type=F32)

    q_ref[...] = mm(C_Q, C_K)
    k_ref[...] = mm(C_K, C_V)
    v_ref[...] = mm(C_V, C_BCH)
    z = mm(C_BCH, C_PC)
    bb_ref[...] = z[:, :WIDTH_B]
    u_ref[...] = z[:, WIDTH_B:2 * WIDTH_B] * z[:, 2 * WIDTH_B:]
    pc_ref[...] = mm(C_PC, C_G)
    step = 512
    for j in range(N_BRANCHES * D_MODEL // step):
        gt_ref[:, j * step:(j + 1) * step] = jax.nn.sigmoid(mm(C_G + j * step, C_G + (j + 1) * step))
    lf_ref[...] = _log_sigmoid(mm(C_F, N_PACK) + bf_ref[...])


def _in_proj(x, mod, g, w_pack, bf_pad, tm, rows_per_mod):
    t = x.shape[0]
    widths = (WIDTH_A, WIDTH_A, WIDTH_A, F_PAD, WIDTH_B, WIDTH_B, WIDTH_C, N_BRANCHES * D_MODEL)
    return pl.pallas_call(
        _in_proj_kernel,
        grid=(t // tm,),
        in_specs=[pl.BlockSpec((tm, D_MODEL), lambda i: (i, 0)),
                  _mod_spec(tm, rows_per_mod),
                  pl.BlockSpec((1, D_MODEL), lambda i: (0, 0)),
                  pl.BlockSpec((D_MODEL, N_PACK), lambda i: (0, 0)),
                  pl.BlockSpec((1, F_PAD), lambda i: (0, 0))],
        out_specs=[pl.BlockSpec((tm, w), lambda i: (i, 0)) for w in widths],
        out_shape=[jax.ShapeDtypeStruct((t, w), F32) for w in widths],
        compiler_params=_cparams(("arbitrary",)),
        name="in_proj",
    )(x, mod, g, w_pack, bf_pad)


LOG2E = 1.4426950408889634
N_PIECES = 3
CUM_BLOCK = 256


def _split3(x):
    hi = x.astype(BF16).astype(F32)
    r = x - hi
    mid = r.astype(BF16).astype(F32)
    lo = (r - mid).astype(BF16).astype(F32)
    return hi, mid, lo


def _forget_bias_kernel(lf_ref, tril_ref, pq_ref, pk_ref, oq_ref, ok_ref, exq_ref, exk_ref):
    tril = tril_ref[...]
    carry = jnp.zeros((1, LANES), F32)
    for blk in range(lf_ref.shape[0] // CUM_BLOCK):
        rows = slice(blk * CUM_BLOCK, (blk + 1) * CUM_BLOCK)
        c = carry
        for piece in _split3(lf_ref[rows, :]):
            c = c + jnp.dot(tril, piece.astype(BF16), preferred_element_type=F32)
        carry = c[CUM_BLOCK - 1:CUM_BLOCK, :]
        pieces = jnp.concatenate([p.astype(BF16) for p in _split3(c * LOG2E)], axis=1)
        for hp in range(N_HEADS // 2):
            exq_ref[hp, rows, :] = jnp.dot(pieces, pq_ref[hp], preferred_element_type=F32) + oq_ref[...]
            exk_ref[hp, rows, :] = jnp.dot(pieces, pk_ref[hp], preferred_element_type=F32) + ok_ref[...]


def _bias_placement():
    pq =np.zeros((N_HEADS // 2, N_PIECES * LANES, LANES), np.float32)
    pk = np.zeros_like(pq)
    oq = np.zeros((1, LANES), np.float32)
    ok = np.zeros((1, LANES), np.float32)
    for hp in range(N_HEADS // 2):
        for side in range(2):
            spare = HEAD_DIM * (1 - side)
            for piece in range(N_PIECES):
                src = piece * LANES + 2 * hp + side
                pq[hp, src, spare + piece] = 1.0
                pk[hp, src, spare + N_PIECES + piece] = -1.0
    for side in range(2):
        spare = HEAD_DIM * (1 - side)
        oq[0, spare + N_PIECES:spare + 2 * N_PIECES] = 1.0
        ok[0, spare:spare + N_PIECES] = 1.0
    return (jnp.asarray(pq, BF16), jnp.asarray(pk, BF16), jnp.asarray(oq), jnp.asarray(ok))


def _forget_bias(lfpad, batch, seq):
    pos = jnp.arange(CUM_BLOCK)
    tril = (pos[:, None] >= pos[None, :]).astype(BF16)
    pq, pk, oq, ok = _bias_placement()
    n_pairs = N_HEADS // 2
    const = lambda shape: pl.BlockSpec(shape, lambda b, n=len(shape): (0,) * n)
    out_spec = pl.BlockSpec((n_pairs, seq, LANES), lambda b: (0, b, 0))
    out_shape = jax.ShapeDtypeStruct((n_pairs, batch * seq, LANES), F32)
    return pl.pallas_call(
        _forget_bias_kernel,
        grid=(batch,),
        in_specs=[pl.BlockSpec((seq, LANES), lambda b: (b, 0)),
                  const(tril.shape), const(pq.shape), const(pk.shape), const(oq.shape), const(ok.shape)],
        out_specs=[out_spec, out_spec],
        out_shape=[out_shape, out_shape],
        compiler_params=_cparams(("arbitrary",)),
        name="forget_bias",
    )(lfpad, tril, pq, pk, oq, ok)


def _attn_kernel(qi_ref, kj_ref, q_ref, k_ref, v_ref, exq_ref, exk_ref, o_ref,
                 qa_sc, m_sc, l_sc, acc_sc, *, tq, tk):
    qi = qi_ref[pl.program_id(2)]
    kj = kj_ref[pl.program_id(2)]
    lane = lax.broadcasted_iota(jnp.int32, (1, LANES), 1)
    first = lane < HEAD_DIM

    @pl.when(kj == 0)
    def _():
        qs = q_ref[...] * (HEAD_DIM ** -0.5 * LOG2E)
        ex = exq_ref[0]
        qa_sc[0] = jnp.where(first, qs, ex).astype(BF16)
        qa_sc[1] = jnp.where(first, ex, qs).astype(BF16)
        m_sc[...] = jnp.full(m_sc.shape, NEG_BIG, F32)
        l_sc[...] = jnp.zeros(l_sc.shape, F32)
        acc_sc[...] = jnp.zeros(acc_sc.shape, F32)

    def block(masked):
        kx = k_ref[...]
        ex = exk_ref[0]
        vb = v_ref[...].astype(BF16)
        if masked:
            key_pos = kj * tk + lax.broadcasted_iota(jnp.int32, (tk, 1), 0)
            q_pos = qi * tq + lax.broadcasted_iota(jnp.int32, (1, tq), 1)
            causal = key_pos <= q_pos
        for h in range(2):
            ka = (jnp.where(first, kx, ex) if h == 0 else jnp.where(first, ex, kx)).astype(BF16)
            s = lax.dot_general(ka, qa_sc[h], (((1,), (1,)), ((), ())),
                                preferred_element_type=F32)
            if masked:
                s = jnp.where(causal, s, NEG_BIG)
            m_prev = m_sc[h]
            m_new = jnp.maximum(m_prev, jnp.max(s, axis=0, keepdims=True))
            alpha = jnp.exp2(m_prev - m_new)
            p = jnp.exp2(s - m_new)
            l_sc[h] = alpha * l_sc[h] + jnp.sum(p, axis=0, keepdims=True)
            pv = lax.dot_general(vb, p.astype(BF16), (((0,), (0,)), ((), ())),
                                 preferred_element_type=F32)
            acc_sc[h] = acc_sc[h] * alpha + pv
            m_sc[h] = m_new

    @pl.when(kj < qi)
    def _():
        block(False)

    @pl.when(kj == qi)
    def _():
        block(True)
        row = lax.broadcasted_iota(jnp.int32, (LANES, 1), 0)
        o_t = jnp.where(row < HEAD_DIM, acc_sc[0] / l_sc[0], acc_sc[1] / l_sc[1])
        o_ref[...] = o_t.T


def _prompt_attention(q, k, v, exq, exk, batch, seq):
    tq = tk = TQ
    nq = seq // tq
    pairs = [(i, j) for i in range(nq) for j in range(i + 1)]
    qi_tab = jnp.asarray([p[0] for p in pairs], jnp.int32)
    kj_tab = jnp.asarray([p[1] for p in pairs], jnp.int32)

    q_map = lambda b, hp, t, qi, kj: (b * nq + qi[t], hp)
    kv_map = lambda b, hp, t, qi, kj: (b * nq + kj[t], hp)
    grid_spec = pltpu.PrefetchScalarGridSpec(
        num_scalar_prefetch=2,
        grid=(batch, N_HEADS // 2, len(pairs)),
        in_specs=[pl.BlockSpec((tq, LANES), q_map),
                  pl.BlockSpec((tk, LANES), kv_map),
                  pl.BlockSpec((tk, LANES), kv_map),
                  pl.BlockSpec((1, tq, LANES), lambda b, hp, t, qi, kj: (hp, b * nq + qi[t], 0)),
                  pl.BlockSpec((1, tk, LANES), lambda b, hp, t, qi, kj: (hp, b * nq + kj[t], 0))],
        out_specs=pl.BlockSpec((tq, LANES), q_map),
        scratch_shapes=[pltpu.VMEM((2, tq, LANES), BF16),
                        pltpu.VMEM((2, 1, tq), F32), pltpu.VMEM((2, 1, tq), F32),
                        pltpu.VMEM((2, LANES, tq), F32)])
    return pl.pallas_call(
        functools.partial(_attn_kernel, tq=tq, tk=tk),
        grid_spec=grid_spec,
        out_shape=jax.ShapeDtypeStruct((batch * seq, WIDTH_A), F32),
        compiler_params=_cparams(("arbitrary",) * 3),
        name="prompt_attention",
    )(qi_tab, kj_tab, q, k, v, exq, exk)


def _zero_fill_padding(ends_ref, xs_ref, zero_sc, sem):
    def zero_tile(start):
        return pltpu.make_async_copy(zero_sc, xs_ref.at[pl.ds(pl.multiple_of(start, TE), TE), :], sem)

    zero_sc[...] = jnp.zeros(zero_sc.shape, F32)
    n_tiles = xs_ref.shape[0] // TE
    n_used = ends_ref[N_EXPERTS - 1] // TE
    for phase in ("start", "wait"):
        prev = 0
        for e in range(N_EXPERTS):
            end = ends_ref[e]

            @pl.when(end > prev)
            def _():
                getattr(zero_tile(end - TE), phase)()
            prev = end

        def unused(t, c):
            getattr(zero_tile(t * TE), phase)()
            return c

        lax.fori_loop(n_used, n_tiles, unused, 0)


def _decode_kernel(pt_ref, q_ref, kn_ref, vn_ref, cq_ref, tri_ref, *rest, n_pages):
    k_refs = rest[:n_pages]
    v_refs = rest[n_pages:2 * n_pages]
    lf_refs = rest[2 * n_pages:3 * n_pages]
    o_ref = rest[3 * n_pages]
    m_sc, l_sc, acc_sc, carry_sc = rest[3 * n_pages + 1:]
    j = pl.program_id(1)
    nj = pl.num_programs(1)
    head_of_lane = lax.broadcasted_iota(jnp.int32, (N_HEADS, WIDTH_A), 1) // HEAD_DIM
    own = head_of_lane == lax.broadcasted_iota(jnp.int32, (N_HEADS, WIDTH_A), 0)

    @pl.when(j == 0)
    def _():
        m_sc[...] = jnp.full(m_sc.shape, NEG_BIG, F32)
        l_sc[...] = jnp.zeros(l_sc.shape, F32)
        acc_sc[...] = jnp.zeros(acc_sc.shape, F32)
        carry_sc[...] = jnp.zeros(carry_sc.shape, F32)

    qbd_f = jnp.where(own, q_ref[0] * (HEAD_DIM ** -0.5), 0.0)
    qbd = qbd_f.astype(BF16)
    cq = cq_ref[0]

    lf = jnp.concatenate([r[0, 0] for r in lf_refs], axis=0)
    hi = lf.astype(BF16)
    r1 = lf - hi.astype(F32)
    mid = r1.astype(BF16)
    lo = (r1 - mid.astype(F32)).astype(BF16)
    tri = tri_ref[...]
    sums = (jnp.dot(hi, tri, preferred_element_type=F32)
            + jnp.dot(mid, tri, preferred_element_type=F32)
            + jnp.dot(lo, tri, preferred_element_type=F32))
    later = sums[:, :PAGE_SIZE]
    total = sums[:, PAGE_SIZE:]

    carry = carry_sc[...]
    bias = [None] * n_pages
    for i in reversed(range(n_pages)):
        rows = slice(i * N_HEADS, (i + 1) * N_HEADS)
        bias[i] = later[rows] + carry + cq
        carry = carry + total[rows]
    carry_sc[...] = carry

    def pair(refs, g):
        return jnp.concatenate([refs[2 * g][0, 0], refs[2 * g + 1][0, 0]], axis=1).astype(BF16)

    n_pairs = n_pages // 2
    scores = [jnp.dot(qbd, pair(k_refs, g), preferred_element_type=F32)
              + jnp.concatenate([bias[2 * g], bias[2 * g + 1]], axis=1) for g in range(n_pairs)]
    s_max = scores[0]
    for s in scores[1:]:
        s_max = jnp.maximum(s_max, s)
    m_prev = m_sc[...]
    m_new = jnp.maximum(m_prev, jnp.max(s_max, axis=-1, keepdims=True))
    alpha = jnp.exp(m_prev - m_new)
    acc = acc_sc[...] * alpha
    p_sum = jnp.zeros((N_HEADS, 2 * PAGE_SIZE), F32)
    for g in range(n_pairs):
        p = jnp.exp(scores[g] - m_new)
        p_sum = p_sum + p
        acc = acc + lax.dot_general(p.astype(BF16), pair(v_refs, g), (((1,), (1,)), ((), ())),
                                    preferred_element_type=F32)
    l_new = l_sc[...] * alpha + jnp.sum(p_sum, axis=-1, keepdims=True)
    m_sc[...] = m_new
    l_sc[...] = l_new
    acc_sc[...] = acc

    @pl.when(j == nj - 1)
    def _():
        s_new = jnp.sum(qbd_f * kn_ref[0], axis=-1, keepdims=True)
        m_f = jnp.maximum(m_new, s_new)
        a1 = jnp.exp(m_new - m_f)
        a2 = jnp.exp(s_new - m_f)
        o = (acc * a1 + a2 * vn_ref[0]) / (l_new * a1 + a2)
        o_ref[0] = jnp.sum(jnp.where(own, o, 0.0), axis=0, keepdims=True)


def _decode_attention(layer, q, k_new, v_new, cq, cache_kt, cache_vt, cache_lft, page_table, tri):
    nb, n_pages_total = page_table.shape
    npg = NP_STEP
    nj = n_pages_total // npg

    def page_map(i):
        def f(b, j, pt):
            return (layer, pt[b, (nj - 1 - j) * npg + i], 0, 0)
        return f

    row = lambda b, j, pt: (b, 0, 0)
    vec = pl.BlockSpec((1, 1, WIDTH_A), row)
    in_specs = [vec, vec, vec, pl.BlockSpec((1, N_HEADS, 1), row),
                pl.BlockSpec((PAGE_SIZE, 2 * PAGE_SIZE), lambda b, j, pt: (0, 0))]
    in_specs += [pl.BlockSpec((1, 1, WIDTH_A, PAGE_SIZE), page_map(i)) for i in range(npg)]
    in_specs += [pl.BlockSpec((1, 1, WIDTH_A, PAGE_SIZE), page_map(i)) for i in range(npg)]
    in_specs += [pl.BlockSpec((1, 1, N_HEADS, PAGE_SIZE), page_map(i)) for i in range(npg)]
    grid_spec = pltpu.PrefetchScalarGridSpec(
        num_scalar_prefetch=1,
        grid=(nb, nj),
        in_specs=in_specs,
        out_specs=vec,
        scratch_shapes=[pltpu.VMEM((N_HEADS, 1), F32),
                        pltpu.VMEM((N_HEADS, 1), F32),
                        pltpu.VMEM((N_HEADS, WIDTH_A), F32),
                        pltpu.VMEM((N_HEADS, PAGE_SIZE), F32)])
    return pl.pallas_call(
        functools.partial(_decode_kernel, n_pages=npg),
        grid_spec=grid_spec,
        out_shape=jax.ShapeDtypeStruct((nb, 1, WIDTH_A), F32),
        compiler_params=_cparams(("arbitrary", "arbitrary")),
        name="decode_attention",
    )(page_table, q, k_new, v_new, cq, tri,
      *([cache_kt] * npg), *([cache_vt] * npg), *([cache_lft] * npg))


def _pool_select(s2, s4, s8, s16):
    lane = lax.broadcasted_iota(jnp.int32, (1, WIDTH_C), 1)
    g = lane // GROUP_DIM_C
    return jnp.where(g == 0, s2, jnp.where(g == 1, s4, jnp.where(g == 2, s8, s16)))


def _merge_tail(a, b_out, d, gt_ref, x, mod_ref, wa_ref, wb_ref, wc_ref, wo_ref, pw_ref, ps_ref, gp_ref):
    c_out = _bdot(d, pw_ref[...]) * ps_ref[...]
    merged = (gt_ref[:, 0:D_MODEL] * _bdot(a, wa_ref[...])
              + gt_ref[:, D_MODEL:2 * D_MODEL] * _bdot(b_out, wb_ref[...])
              + gt_ref[:, 2 * D_MODEL:] * _bdot(c_out, wc_ref[...]))
    m = _bdot(merged, wo_ref[...])
    return x + mod_ref[:, 2, :] * _rms(m, gp_ref[...])


def _mix_prompt_kernel(a_ref, bb_ref, u_ref, pc_ref, uh_ref, ph_ref, gt_ref, x_ref, mod_ref,
                       cw_ref, wa_ref, wb_ref, wc_ref, wo_ref, pw_ref, ps_ref, gp_ref,
                       o_ref, ue_sc, pe_sc, *, tm, tiles_per_seq):
    i = pl.program_id(0)
    keep = jnp.where((i % tiles_per_seq) == 0, 0.0, 1.0)
    hu, hp = SUBLANES, 2 * SUBLANES
    ue_sc[0:hu, :] = uh_ref[...] * keep
    ue_sc[hu:, :] = u_ref[...]
    pe_sc[0:hp, :] = ph_ref[...] * keep
    pe_sc[hp:, :] = pc_ref[...]
    y_conv = (ue_sc[hu - 2:hu - 2 + tm, :] * cw_ref[0:1, :]
              + ue_sc[hu - 1:hu - 1 + tm, :] * cw_ref[1:2, :]
              + u_ref[...] * cw_ref[2:3, :])
    b_out = bb_ref[...] * y_conv
    p = pc_ref[...]

    def back(jj):
        return pe_sc[hp - jj:hp - jj + tm, :]

    s2 = p + back(1)
    s4 = s2 + back(2) + back(3)
    s8 = s4 + back(4) + back(5) + back(6) + back(7)
    s16 = s8
    for jj in range(8, 16):
        s16 = s16 + back(jj)
    pos = (i % tiles_per_seq) * tm + lax.broadcasted_iota(jnp.int32, (tm, 1), 0)
    posf = (pos + 1).astype(F32)
    means = _pool_select(s2 / jnp.minimum(2.0, posf), s4 / jnp.minimum(4.0, posf),
                         s8 / jnp.minimum(8.0, posf), s16 / jnp.minimum(16.0, posf))
    d = means - p
    o_ref[...] = _merge_tail(a_ref[...], b_out, d, gt_ref, x_ref[...], mod_ref,
                             wa_ref, wb_ref, wc_ref, wo_ref, pw_ref, ps_ref, gp_ref)


def _mix_sample_kernel(a_ref, bb_ref, u_ref, pc_ref, ch_ref, ph_ref, gt_ref, x_ref, mod_ref,
                       cw_ref, wa_ref, wb_ref, wc_ref, wo_ref, pw_ref, ps_ref, gp_ref, o_ref):
    u = u_ref[...]
    y_conv = ch_ref[:, 0, :] * cw_ref[0:1, :] + ch_ref[:, 1, :] * cw_ref[1:2, :] + u * cw_ref[2:3, :]
    b_out = bb_ref[...] * y_conv
    p = pc_ref[...]
    sums = []
    acc = p
    nxt = POOL_HIST - 1
    for w in POOL_WINDOWS:
        while POOL_HIST - nxt < w:
            acc = acc + ph_ref[:, nxt, :]
            nxt -= 1
        sums.append(acc / float(w))
    d = _pool_select(*sums) - p
    o_ref[...] = _merge_tail(a_ref[...], b_out, d, gt_ref, x_ref[...], mod_ref,
                             wa_ref, wb_ref, wc_ref, wo_ref, pw_ref, ps_ref, gp_ref)


def _const_specs(shapes):
    return [pl.BlockSpec(s, lambda i, n=len(s): (0,) * n) for s in shapes]


def _mix_prompt(a, bb, u, pc, gt, x, mod, wts, tm, seq):
    t = x.shape[0]
    tiles_per_seq = seq // tm
    hu, hp = SUBLANES, 2 * SUBLANES
    row = lambda w: pl.BlockSpec((tm, w), lambda i: (i, 0))
    in_specs = [row(WIDTH_A), row(WIDTH_B), row(WIDTH_B), row(WIDTH_C),
                pl.BlockSpec((hu, WIDTH_B), lambda i: (jnp.maximum(i * (tm // hu) - 1, 0), 0)),
                pl.BlockSpec((hp, WIDTH_C), lambda i: (jnp.maximum(i * (tm // hp) - 1, 0), 0)),
                row(N_BRANCHES * D_MODEL), row(D_MODEL), _mod_spec(tm, seq)]
    in_specs += _const_specs([w.shape for w in wts])
    return pl.pallas_call(
        functools.partial(_mix_prompt_kernel, tm=tm, tiles_per_seq=tiles_per_seq),
        grid=(t // tm,),
        in_specs=in_specs,
        out_specs=row(D_MODEL),
        out_shape=jax.ShapeDtypeStruct((t, D_MODEL), F32),
        scratch_shapes=[pltpu.VMEM((hu + tm, WIDTH_B), F32), pltpu.VMEM((hp + tm, WIDTH_C), F32)],
        compiler_params=_cparams(("arbitrary",)),
        name="mix_prompt",
    )(a, bb, u, pc, u, pc, gt, x, mod, *wts)


def _mix_sample(a, bb, u, pc, conv_hist, pool_hist, gt, x, mod, wts):
    t = x.shape[0]
    shapes = [(t, WIDTH_A), (t, WIDTH_B), (t, WIDTH_B), (t, WIDTH_C), conv_hist.shape, pool_hist.shape,
              (t, N_BRANCHES * D_MODEL), (t, D_MODEL), (t, N_MOD, D_MODEL)] + [w.shape for w in wts]
    return pl.pallas_call(
        _mix_sample_kernel,
        grid=(1,),
        in_specs=_const_specs(shapes),
        out_specs=pl.BlockSpec((t, D_MODEL), lambda i: (0, 0)),
        out_shape=jax.ShapeDtypeStruct((t, D_MODEL), F32),
        compiler_params=_cparams(("arbitrary",)),
        name="mix_sample",
    )(a, bb, u, pc, conv_hist, pool_hist, gt, x, mod, *wts)


def _first_index(mask, idx, big, axis):
    return jnp.min(jnp.where(mask, idx, big), axis=axis, keepdims=True)


def _router_kernel(x_ref, mod_ref, g_ref, wr_ref, rb_ref, upper_ref,
                   h_ref, eid_ref, wt_ref, rank_ref, comb_ref, cnt_ref, carry_sc, *, tm):
    i = pl.program_id(0)

    @pl.when(i == 0)
    def _():
        carry_sc[...] = jnp.zeros(carry_sc.shape, F32)

    h = _rms(x_ref[...], g_ref[...]) * (1.0 + mod_ref[:, 4, :]) + mod_ref[:, 3, :]
    h_ref[...] = h
    logits = lax.dot_general(wr_ref[...], h.astype(BF16), (((1,), (1,)), ((), ())),
                             preferred_element_type=F32)
    s = jax.nn.sigmoid(logits)
    sel = s + rb_ref[...]
    sg = sel.reshape(N_EXPERT_GROUPS, GROUP_SIZE, tm)
    in_idx = lax.broadcasted_iota(jnp.int32, sg.shape, 1)
    top1 = jnp.max(sg, axis=1, keepdims=True)
    f1 = _first_index(sg == top1, in_idx, GROUP_SIZE, 1)
    top2 = jnp.max(jnp.where(in_idx == f1, -jnp.inf, sg), axis=1, keepdims=True)
    gscore = top1 + top2
    g_idx = lax.broadcasted_iota(jnp.int32, gscore.shape, 0)
    gsel = jnp.zeros(gscore.shape, F32)
    for _ in range(TOPK_GROUPS):
        mx = jnp.max(gscore, axis=0, keepdims=True)
        hit = g_idx == _first_index(gscore == mx, g_idx, N_EXPERT_GROUPS, 0)
        gsel = jnp.where(hit, 1.0, gsel)
        gscore = jnp.where(hit, -jnp.inf, gscore)
    masked = jnp.where(gsel > 0.5, sg, -jnp.inf).reshape(N_EXPERTS, tm)
    e_idx = lax.broadcasted_iota(jnp.int32, masked.shape, 0)
    hits, ids, wts = [], [], []
    for _ in range(TOP_K):
        mx = jnp.max(masked, axis=0, keepdims=True)
        fe = _first_index(masked == mx, e_idx, N_EXPERTS, 0)
        hit = e_idx == fe
        hits.append(hit)
        ids.append(fe)
        wts.append(jnp.sum(jnp.where(hit, s, 0.0), axis=0, keepdims=True))
        masked = jnp.where(hit, -jnp.inf, masked)
    wsum = wts[0]
    for w in wts[1:]:
        wsum = wsum + w
    wts = [w / wsum * ROUTED_SCALE for w in wts]
    onehot = jnp.zeros((N_EXPERTS, tm), F32)
    comb = jnp.zeros((N_EXPERTS, tm), F32)
    for hit, w in zip(hits, wts):
        onehot = onehot + jnp.where(hit, 1.0, 0.0)
        comb = comb + jnp.where(hit, w, 0.0)
    before = jnp.dot(onehot.astype(BF16), upper_ref[...], preferred_element_type=F32) + carry_sc[...]
    ranks = [jnp.sum(jnp.where(hit, before, 0.0), axis=0, keepdims=True) for hit in hits]
    carry_new = carry_sc[...] + jnp.sum(onehot, axis=1, keepdims=True)
    carry_sc[...] = carry_new
    pad = jnp.zeros((SUBLANES - TOP_K, tm), F32)
    eid_ref[...] = jnp.concatenate(ids + [pad.astype(jnp.int32)], axis=0)
    wt_ref[...] = jnp.concatenate(wts + [pad], axis=0)
    rank_ref[...] = jnp.concatenate(ranks + [pad], axis=0).astype(jnp.int32)
    comb_ref[...] = comb
    cnt_ref[...] = carry_new.astype(jnp.int32)


def _router(x, mod, g, wr_t, rb, upper, tm, rows_per_mod):
    t = x.shape[0]
    col = lambda rows: pl.BlockSpec((rows, tm), lambda i: (0, i))
    return pl.pallas_call(
        functools.partial(_router_kernel, tm=tm),
        grid=(t // tm,),
        in_specs=[pl.BlockSpec((tm, D_MODEL), lambda i: (i, 0)),
                  _mod_spec(tm, rows_per_mod),
                  pl.BlockSpec((1, D_MODEL), lambda i: (0, 0)),
                  pl.BlockSpec((N_EXPERTS, D_MODEL), lambda i: (0, 0)),
                  pl.BlockSpec((N_EXPERTS, 1), lambda i: (0, 0)),
                  pl.BlockSpec((tm, tm), lambda i: (0, 0))],
        out_specs=[pl.BlockSpec((tm, D_MODEL), lambda i: (i, 0)),
                   col(SUBLANES), col(SUBLANES), col(SUBLANES), col(N_EXPERTS),
                   pl.BlockSpec((N_EXPERTS, 1), lambda i: (0, 0))],
        out_shape=[jax.ShapeDtypeStruct((t, D_MODEL), F32),
                   jax.ShapeDtypeStruct((SUBLANES, t), jnp.int32),
                   jax.ShapeDtypeStruct((SUBLANES, t), F32),
                   jax.ShapeDtypeStruct((SUBLANES, t), jnp.int32),
                   jax.ShapeDtypeStruct((N_EXPERTS, t), F32),
                   jax.ShapeDtypeStruct((N_EXPERTS, 1), jnp.int32)],
        scratch_shapes=[pltpu.VMEM((N_EXPERTS, 1), F32)],
        compiler_params=_cparams(("arbitrary",)),
        name="router",
    )(x, mod, g, wr_t, rb, upper)


def _row_copy_wait(src_tile, dst_tile, sem):
    pltpu.make_async_copy(src_tile, dst_tile, sem).wait()


def _dispatch_kernel(ends_ref, dest_ref, h_ref, o_ref, zero_sc, sem, *, tm):
    @pl.when(pl.program_id(0) == 0)
    def _():
        _zero_fill_padding(ends_ref, o_ref, zero_sc, sem)

    def rows(r0, c):
        for dr in range(DMA_UNROLL):
            r = r0 * DMA_UNROLL + dr
            for k in range(TOP_K):
                d = dest_ref[0, 0, r * TOP_K + k]
                pltpu.make_async_copy(h_ref.at[pl.ds(r, 1), :], o_ref.at[pl.ds(d, 1), :],
                                      sem).start(priority=(dr * TOP_K + k) % 2)
        return c

    lax.fori_loop(0, tm // DMA_UNROLL, rows, 0)
    for _ in range(TOP_K):
        _row_copy_wait(h_ref, o_ref.at[pl.ds(0, tm), :], sem)


def _dispatch(h, dest_tiles, ends, n_rows_pad, tm):
    t = h.shape[0]
    grid_spec = pltpu.PrefetchScalarGridSpec(
        num_scalar_prefetch=1,
        grid=(t // tm,),
        in_specs=[pl.BlockSpec((1, 1, tm * TOP_K), lambda i, ends: (i, 0, 0), memory_space=pltpu.SMEM),
                  pl.BlockSpec((tm, D_MODEL), lambda i, ends: (i, 0))],
        out_specs=pl.BlockSpec(memory_space=pl.ANY),
        scratch_shapes=[pltpu.VMEM((TE, D_MODEL), F32), pltpu.SemaphoreType.DMA(())])
    return pl.pallas_call(
        functools.partial(_dispatch_kernel, tm=tm),
        grid_spec=grid_spec,
        out_shape=jax.ShapeDtypeStruct((n_rows_pad, D_MODEL), F32),
        compiler_params=_cparams(("arbitrary",)),
        name="dispatch",
    )(ends, dest_tiles, h)


def _expert_kernel(te_ref, nv_ref, x_ref, wg_ref, wu_ref, wd_ref, y_ref, wgu_sc, wd_sc):
    t = pl.program_id(0)
    prev = te_ref[jnp.maximum(t - 1, 0)]
    fresh = jnp.logical_or(t == 0, te_ref[t] != prev)

    @pl.when(fresh)
    def _():
        wgu_sc[:, 0:D_EXPERT] = wg_ref[0, 0].astype(BF16)
        wgu_sc[:, D_EXPERT:] = wu_ref[0, 0].astype(BF16)
        wd_sc[...] = wd_ref[0, 0].astype(BF16)

    @pl.when(t < nv_ref[0])
    def _():
        gu = jnp.dot(x_ref[...].astype(BF16), wgu_sc[...], preferred_element_type=F32)
        a = _silu(gu[:, :D_EXPERT]) * gu[:, D_EXPERT:]
        y_ref[...] = jnp.dot(a.astype(BF16), wd_sc[...], preferred_element_type=F32)

    @pl.when(t >= nv_ref[0])
    def _():
        y_ref[...] = jnp.zeros(y_ref.shape, F32)


def _experts(layer, xs, tile_expert, n_valid, we_gate, we_up, we_down):
    n_tiles = xs.shape[0] // TE
    rows = lambda t, te, nv: (t, 0)
    wmap = lambda t, te, nv: (layer, te[t], 0, 0)
    grid_spec = pltpu.PrefetchScalarGridSpec(
        num_scalar_prefetch=2,
        grid=(n_tiles,),
        in_specs=[pl.BlockSpec((TE, D_MODEL), rows),
                  pl.BlockSpec((1, 1, D_MODEL, D_EXPERT), wmap),
                  pl.BlockSpec((1, 1, D_MODEL, D_EXPERT), wmap),
                  pl.BlockSpec((1, 1, D_EXPERT, D_MODEL), wmap)],
        out_specs=pl.BlockSpec((TE, D_MODEL), rows),
        scratch_shapes=[pltpu.VMEM((D_MODEL, 2 * D_EXPERT), BF16),
                        pltpu.VMEM((D_EXPERT, D_MODEL), BF16)])
    return pl.pallas_call(
        _expert_kernel,
        grid_spec=grid_spec,
        out_shape=jax.ShapeDtypeStruct((xs.shape[0], D_MODEL), F32),
        compiler_params=_cparams(("arbitrary",)),
        name="experts",
    )(tile_expert, n_valid, xs, we_gate, we_up, we_down)


def _dense_expert_kernel(h_ref, comb_ref, wg_ref, wu_ref, wd_ref, o_ref, acc_sc):
    e = pl.program_id(0)

    @pl.when(e == 0)
    def _():
        acc_sc[...] = jnp.zeros(acc_sc.shape, F32)

    hb = h_ref[...].astype(BF16)
    lane = lax.broadcasted_iota(jnp.int32, comb_ref.shape, 1)
    ce = jnp.sum(jnp.where(lane == e, comb_ref[...], 0.0), axis=1, keepdims=True)
    g = jnp.dot(hb, wg_ref[0, 0].astype(BF16), preferred_element_type=F32)
    u = jnp.dot(hb, wu_ref[0, 0].astype(BF16), preferred_element_type=F32)
    a = _silu(g) * u * ce
    acc_sc[...] += jnp.dot(a.astype(BF16), wd_ref[0, 0].astype(BF16), preferred_element_type=F32)

    @pl.when(e == pl.num_programs(0) - 1)
    def _():
        o_ref[...] = acc_sc[...]


def _dense_experts(layer, h, comb, we_gate, we_up, we_down):
    t = h.shape[0]
    wmap = lambda e: (layer, e, 0, 0)
    return pl.pallas_call(
        _dense_expert_kernel,
        grid=(N_EXPERTS,),
        in_specs=[pl.BlockSpec((t, D_MODEL), lambda e: (0, 0)),
                  pl.BlockSpec((t, N_EXPERTS), lambda e: (0, 0)),
                  pl.BlockSpec((1, 1, D_MODEL, D_EXPERT), wmap),
                  pl.BlockSpec((1, 1, D_MODEL, D_EXPERT), wmap),
                  pl.BlockSpec((1, 1, D_EXPERT, D_MODEL), wmap)],
        out_specs=pl.BlockSpec((t, D_MODEL), lambda e: (0, 0)),
        out_shape=jax.ShapeDtypeStruct((t, D_MODEL), F32),
        scratch_shapes=[pltpu.VMEM((t, D_MODEL), F32)],
        compiler_params=_cparams(("arbitrary",)),
        name="dense_experts",
    )(h, comb, we_gate, we_up, we_down)


def _shared_expert(h_ref, wgu_ref, wd_ref):
    gu = _bdot(h_ref[...], wgu_ref[...])
    a = _silu(gu[:, :D_EXPERT]) * gu[:, D_EXPERT:]
    return _bdot(a, wd_ref[...])


def _post_ffn(y, x_ref, mod_ref, g_ref):
    return x_ref[...] + mod_ref[:, 5, :] * _rms(y, g_ref[...])


def _ffn_tail(routed, x_ref, h_ref, mod_ref, wgu_ref, wd_ref, g_ref):
    return _post_ffn(routed + _shared_expert(h_ref, wgu_ref, wd_ref), x_ref, mod_ref, g_ref)


def _ffn_out_kernel(x_ref, r_ref, h_ref, mod_ref, wgu_ref, wd_ref, g_ref, o_ref):
    o_ref[...] = _ffn_tail(r_ref[...], x_ref, h_ref, mod_ref, wgu_ref, wd_ref, g_ref)


def _ffn_out(x, routed, h, mod, ws_gu, ws_d, g, tm, rows_per_mod):
    t = x.shape[0]
    row = pl.BlockSpec((tm, D_MODEL), lambda i: (i, 0))
    return pl.pallas_call(
        _ffn_out_kernel,
        grid=(t // tm,),
        in_specs=[row, row, row, _mod_spec(tm, rows_per_mod),
                  pl.BlockSpec((D_MODEL, 2 * D_EXPERT), lambda i: (0, 0)),
                  pl.BlockSpec((D_EXPERT, D_MODEL), lambda i: (0, 0)),
                  pl.BlockSpec((1, D_MODEL), lambda i: (0, 0))],
        out_specs=row,
        out_shape=jax.ShapeDtypeStruct((t, D_MODEL), F32),
        compiler_params=_cparams(("arbitrary",)),
        name="ffn_out",
    )(x, routed, h, mod, ws_gu, ws_d, g)


def _combine_kernel(dest_ref, y_ref, wt_ref, x_ref, h_ref, mod_ref, wgu_ref, wd_ref, g_ref,
                    o_ref, rows_sc, sem, *, tm):
    def rows(r0, c):
        for dr in range(DMA_UNROLL):
            r = r0 * DMA_UNROLL + dr
            for k in range(TOP_K):
                d = dest_ref[0, 0, r * TOP_K + k]
                pltpu.make_async_copy(y_ref.at[pl.ds(d, 1), :], rows_sc.at[k, pl.ds(r, 1), :],
                                      sem).start(priority=(dr * TOP_K + k) % 2)
        return c

    lax.fori_loop(0, tm // DMA_UNROLL, rows, 0)
    shared = _shared_expert(h_ref, wgu_ref, wd_ref)
    for k in range(TOP_K):
        _row_copy_wait(y_ref.at[pl.ds(0, tm), :], rows_sc.at[k], sem)
    routed = rows_sc[0] * wt_ref[:, 0:1]
    for k in range(1, TOP_K):
        routed = routed + rows_sc[k] * wt_ref[:, k:k + 1]
    o_ref[...] = _post_ffn(routed + shared, x_ref, mod_ref, g_ref)


def _combine_ffn_out(x, y, dest_tiles, wt, h, mod, ws_gu, ws_d, g, tm, rows_per_mod):
    t = x.shape[0]
    row = pl.BlockSpec((tm, D_MODEL), lambda i: (i, 0))
    return pl.pallas_call(
        functools.partial(_combine_kernel, tm=tm),
        grid=(t // tm,),
        in_specs=[pl.BlockSpec((1, 1, tm * TOP_K), lambda i: (i, 0, 0), memory_space=pltpu.SMEM),
                  pl.BlockSpec(memory_space=pl.ANY),
                  pl.BlockSpec((tm, SUBLANES), lambda i: (i, 0)),
                  row, row, _mod_spec(tm, rows_per_mod),
                  pl.BlockSpec((D_MODEL, 2 * D_EXPERT), lambda i: (0, 0)),
                  pl.BlockSpec((D_EXPERT, D_MODEL), lambda i: (0, 0)),
                  pl.BlockSpec((1, D_MODEL), lambda i: (0, 0))],
        out_specs=row,
        out_shape=jax.ShapeDtypeStruct((t, D_MODEL), F32),
        scratch_shapes=[pltpu.VMEM((TOP_K, tm, D_MODEL), F32), pltpu.SemaphoreType.DMA(())],
        compiler_params=_cparams(("arbitrary",)),
        name="combine_ffn_out",
    )(dest_tiles, y, wt, x, h, mod, ws_gu, ws_d, g)


def _pack_w_in(w):
    f = jnp.pad(w[:, OFF_F:OFF_BB], ((0, 0), (0, F_PAD - N_HEADS)))
    return jnp.concatenate([w[:, OFF_Q:OFF_F], w[:, OFF_BB:OFF_PC], w[:, OFF_PC:OFF_G], w[:, OFF_G:], f],
                           axis=1).astype(BF16)


def _block_diag(pool_w):
    out = jnp.zeros((WIDTH_C, WIDTH_C), F32)
    for g in range(len(POOL_WINDOWS)):
        lo = g * GROUP_DIM_C
        out = out.at[lo:lo + GROUP_DIM_C, lo:lo + GROUP_DIM_C].set(pool_w[g])
    return out.astype(BF16)


def kernel(x_prompt, x_sample, cache_k, cache_v, cache_logf, state_conv, state_pool, page_table,
           c_prompt, c_sample, w_in, b_forget, conv_w, pool_w, pool_scale, w_br_attn, w_br_conv,
           w_br_pool, w_out, g_pre_mix, g_post_mix, g_pre_ffn, g_post_ffn, w_ada, b_ada, w_router,
           router_bias, we_gate, we_up, we_down, ws_gate, ws_up, ws_down):
    bp, seq, d = x_prompt.shape
    bd = x_sample.shape[0]
    tp = bp * seq
    n_phys = cache_k.shape[1]

    xp = x_prompt.reshape(tp, d)
    xs = x_sample.reshape(bd, d)
    c_all = jnp.concatenate([c_prompt, c_sample], axis=0)
    cache_kt = cache_k.transpose(0, 1, 3, 4, 2).reshape(DEPTH, n_phys, WIDTH_A, PAGE_SIZE)
    cache_vt = cache_v.transpose(0, 1, 3, 4, 2).reshape(DEPTH, n_phys, WIDTH_A, PAGE_SIZE)
    cache_lft = cache_logf.transpose(0, 1, 3, 2)

    pos = jnp.arange(PAGE_SIZE)
    tri = jnp.concatenate([(pos[:, None] > pos[None, :]).astype(BF16),
                           jnp.ones((PAGE_SIZE, PAGE_SIZE), BF16)], axis=1)
    tok = jnp.arange(TM)
    upper = (tok[:, None] < tok[None, :]).astype(BF16)
    tok_s = jnp.arange(bd)
    upper_s = (tok_s[:, None] < tok_s[None, :]).astype(BF16)

    n_pairs = tp * TOP_K
    n_rows_pad = n_pairs + N_EXPERTS * TE
    n_tiles = n_rows_pad // TE
    experts_iota = jnp.arange(N_EXPERTS, dtype=jnp.int32)

    outs = {name: [] for name in ("kp", "vp", "lfp", "cvp", "plp", "ks", "vs", "lfs", "cvs", "pls")}
    for l in range(DEPTH):
        mod = _modulation(c_all, w_ada[l], b_ada[l]).reshape(bp + bd, N_MOD, d)
        mod_p, mod_s = mod[:bp], mod[bp:]
        w_pack = _pack_w_in(w_in[l])
        bf_pad = jnp.pad(b_forget[l], (0, F_PAD - N_HEADS)).reshape(1, F_PAD)
        mix_w = (conv_w[l], w_br_attn[l].astype(BF16), w_br_conv[l].astype(BF16),
                 w_br_pool[l].astype(BF16), w_out[l].astype(BF16), _block_diag(pool_w[l]),
                 pool_scale[l].reshape(1, -1), g_post_mix[l].reshape(1, -1))
        wr_t = w_router[l].T.astype(BF16)
        rb = router_bias[l].reshape(N_EXPERTS, 1)
        ws_gu = jnp.concatenate([ws_gate[l], ws_up[l]], axis=1).astype(BF16)
        ws_d = ws_down[l].astype(BF16)
        g1 = g_pre_mix[l].reshape(1, -1)
        g3 = g_pre_ffn[l].reshape(1, -1)
        g4 = g_post_ffn[l].reshape(1, -1)

        q, k, v, lfpad, bb, u, pc, gt = _in_proj(xp, mod_p, g1, w_pack, bf_pad, TM, seq)
        lf = lfpad[:, :N_HEADS].reshape(bp, seq, N_HEADS)
        exq, exk = _forget_bias(lfpad, bp, seq)
        a = _prompt_attention(q, k, v, exq, exk, bp, seq)
        xp = _mix_prompt(a, bb, u, pc, gt, xp, mod_p, mix_w, TM, seq)
        h2, eid, wt, rank, _, counts = _router(xp, mod_p, g3, wr_t, rb, upper, TM, seq)
        counts = counts[:, 0]
        padded = ((counts + TE - 1) // TE) * TE
        ends = jnp.cumsum(padded)
        starts = ends - padded
        eid6, rank6 = eid[:TOP_K], rank[:TOP_K]
        start_of = jnp.sum(jnp.where(eid6[:, :, None] == experts_iota, starts, 0), axis=-1)
        dest_tiles = (start_of + rank6).T.reshape(tp // TR, 1, TR * TOP_K)
        tile_start = jnp.arange(n_tiles, dtype=jnp.int32) * TE
        n_valid = ends[-1] // TE
        tile_expert = jnp.sum((tile_start[:, None] >= ends[None, :]).astype(jnp.int32), axis=1)
        last_expert = jnp.sum((ends[-1] - TE >= ends).astype(jnp.int32))
        tile_expert = jnp.minimum(tile_expert, last_expert)
        xs_sorted = _dispatch(h2, dest_tiles, ends, n_rows_pad, TR)
        y = _experts(l, xs_sorted, tile_expert, n_valid.reshape(1), we_gate, we_up, we_down)
        xp = _combine_ffn_out(xp, y, dest_tiles, wt.T, h2, mod_p, ws_gu, ws_d, g4, TR, seq)

        outs["kp"].append(k.reshape(bp, seq, N_HEADS, HEAD_DIM))
        outs["vp"].append(v.reshape(bp, seq, N_HEADS, HEAD_DIM))
        outs["lfp"].append(lf)
        outs["cvp"].append(u.reshape(bp, seq, WIDTH_B)[:, seq - (CONV_WIDTH - 1):])
        outs["plp"].append(pc.reshape(bp, seq, WIDTH_C)[:, seq - POOL_HIST:])

        q, k, v, lfpad, bb, u, pc, gt = _in_proj(xs, mod_s, g1, w_pack, bf_pad, bd, 1)
        lf = lfpad[:, :N_HEADS]
        vec = lambda z: z.reshape(bd, 1, WIDTH_A)
        a = _decode_attention(l, vec(q), vec(k), vec(v), lf.reshape(bd, N_HEADS, 1),
                              cache_kt, cache_vt, cache_lft, page_table, tri).reshape(bd, WIDTH_A)
        xs = _mix_sample(a, bb, u, pc, state_conv[l], state_pool[l], gt, xs, mod_s, mix_w)
        h2, _, _, _, comb, _ = _router(xs, mod_s, g3, wr_t, rb, upper_s, bd, 1)
        routed = _dense_experts(l, h2, comb.T, we_gate, we_up, we_down)
        xs = _ffn_out(xs, routed, h2, mod_s, ws_gu, ws_d, g4, bd, 1)

        outs["ks"].append(k.reshape(bd, 1, N_HEADS, HEAD_DIM))
        outs["vs"].append(v.reshape(bd, 1, N_HEADS, HEAD_DIM))
        outs["lfs"].append(lf.reshape(bd, 1, N_HEADS))
        outs["cvs"].append(jnp.concatenate([state_conv[l][:, 1:], u[:, None, :]], axis=1))
        outs["pls"].append(jnp.concatenate([state_pool[l][:, 1:], pc[:, None, :]], axis=1))

    st = {name: jnp.stack(vals) for name, vals in outs.items()}
    return (xp.reshape(bp, seq, d), xs.reshape(bd, 1, d), st["kp"], st["vp"], st["lfp"], st["cvp"],
            st["plp"], st["ks"], st["vs"], st["lfs"], st["cvs"], st["pls"])
```

```python
import functools

import jax
import jax.numpy as jnp
import numpy as np
from jax import lax
from jax.experimental import pallas as pl
from jax.experimental.pallas import tpu as pltpu

F32 = jnp.float32
BF16 = jnp.bfloat16

D_MODEL = 1024
DEPTH = 2
PAGE_SIZE = 128
N_HEADS = 8
HEAD_DIM = 64
WIDTH_A = N_HEADS * HEAD_DIM
WIDTH_B = 256
CONV_WIDTH = 3
POOL_WINDOWS = (2, 4, 8, 16)
WIDTH_C = 256
GROUP_DIM_C = WIDTH_C // len(POOL_WINDOWS)
POOL_HIST = max(POOL_WINDOWS) - 1
N_BRANCHES = 3
OFF_Q = 0
OFF_K = OFF_Q + WIDTH_A
OFF_V = OFF_K + WIDTH_A
OFF_F = OFF_V + WIDTH_A
OFF_BB = OFF_F + N_HEADS
OFF_CB = OFF_BB + WIDTH_B
OFF_HB = OFF_CB + WIDTH_B
OFF_PC = OFF_HB + WIDTH_B
OFF_G = OFF_PC + WIDTH_C
N_IN = OFF_G + N_BRANCHES * D_MODEL
N_EXPERTS = 64
TOP_K = 6
N_EXPERT_GROUPS = 8
GROUP_SIZE = N_EXPERTS // N_EXPERT_GROUPS
TOPK_GROUPS = 4
D_EXPERT = 256
ROUTED_SCALE = 2.5
N_MOD = 6
RMS_EPS = 1e-6

LANES = 128
SUBLANES = 8
F_PAD = LANES
NEG_BIG = -1e30
VMEM_LIMIT = 56 * 1024 * 1024

C_Q, C_K, C_V = 0, 512, 1024
C_BCH = 1536
C_PC = 2304
C_G = 2560
C_F = C_G + N_BRANCHES * D_MODEL
N_PACK = C_F + F_PAD

TM = 512
TR = 512
TE = 512
TQ = 512
NP_STEP = 8
DMA_UNROLL = 4


def _cparams(sem):
    return pltpu.CompilerParams(dimension_semantics=sem, vmem_limit_bytes=VMEM_LIMIT)


def _rms(x, g):
    ms = jnp.mean(x * x, axis=-1, keepdims=True)
    return x * lax.rsqrt(ms + RMS_EPS) * g


def _silu(x):
    return x * jax.nn.sigmoid(x)


def _log_sigmoid(x):
    return jnp.minimum(x, 0.0) - jnp.log1p(jnp.exp(-jnp.abs(x)))


def _bdot(a, b):
    return jnp.dot(a.astype(BF16), b.astype(BF16), preferred_element_type=F32)


def _mod_spec(tm, rows_per_mod):
    if rows_per_mod > 1:
        return pl.BlockSpec((1, N_MOD, D_MODEL), lambda i: (i // (rows_per_mod // tm), 0, 0))
    return pl.BlockSpec((tm, N_MOD, D_MODEL), lambda i: (i, 0, 0))


def _mod_kernel(c_ref, w_ref, b_ref, o_ref):
    o_ref[...] = _bdot(_silu(c_ref[...]), w_ref[...]) + b_ref[...]


def _modulation(c_all, w_ada, b_ada):
    n = c_all.shape[0]
    tn = 1536
    return pl.pallas_call(
        _mod_kernel,
        grid=(N_MOD * D_MODEL // tn,),
        in_specs=[pl.BlockSpec((n, D_MODEL), lambda j: (0, 0)),
                  pl.BlockSpec((D_MODEL, tn), lambda j: (0, j)),
                  pl.BlockSpec((1, tn), lambda j: (0, j))],
        out_specs=pl.BlockSpec((n, tn), lambda j: (0, j)),
        out_shape=jax.ShapeDtypeStruct((n, N_MOD * D_MODEL), F32),
        compiler_params=_cparams(("arbitrary",)),
        name="modulation",
    )(c_all, w_ada, b_ada.reshape(1, -1))


def _in_proj_kernel(x_ref, mod_ref, g_ref, w_ref, bf_ref,
                    q_ref, k_ref, v_ref, lf_ref, bb_ref, u_ref, pc_ref, gt_ref):
    h = _rms(x_ref[...], g_ref[...]) * (1.0 + mod_ref[:, 1, :]) + mod_ref[:, 0, :]
    hb = h.astype(BF16)

    def mm(lo, hi):
        return jnp.dot(hb, w_ref[:, lo:hi], preferred_element_type=F32)

    q_ref[...] = mm(C_Q, C_K)
    k_ref[...] = mm(C_K, C_V)
    v_ref[...] = mm(C_V, C_BCH)
    z = mm(C_BCH, C_PC)
    bb_ref[...] = z[:, :WIDTH_B]
    u_ref[...] = z[:, WIDTH_B:2 * WIDTH_B] * z[:, 2 * WIDTH_B:]
    pc_ref[...] = mm(C_PC, C_G)
    step = 512
    for j in range(N_BRANCHES * D_MODEL // step):
        gt_ref[:, j * step:(j + 1) * step] = jax.nn.sigmoid(mm(C_G + j * step, C_G + (j + 1) * step))
    lf_ref[...] = _log_sigmoid(mm(C_F, N_PACK) + bf_ref[...])


def _in_proj(x, mod, g, w_pack, bf_pad, tm, rows_per_mod):
    t = x.shape[0]
    widths = (WIDTH_A, WIDTH_A, WIDTH_A, F_PAD, WIDTH_B, WIDTH_B, WIDTH_C, N_BRANCHES * D_MODEL)
    return pl.pallas_call(
        _in_proj_kernel,
        grid=(t // tm,),
        in_specs=[pl.BlockSpec((tm, D_MODEL), lambda i: (i, 0)),
                  _mod_spec(tm, rows_per_mod),
                  pl.BlockSpec((1, D_MODEL), lambda i: (0, 0)),
                  pl.BlockSpec((D_MODEL, N_PACK), lambda i: (0, 0)),
                  pl.BlockSpec((1, F_PAD), lambda i: (0, 0))],
        out_specs=[pl.BlockSpec((tm, w), lambda i: (i, 0)) for w in widths],
        out_shape=[jax.ShapeDtypeStruct((t, w), F32) for w in widths],
        compiler_params=_cparams(("arbitrary",)),
        name="in_proj",
    )(x, mod, g, w_pack, bf_pad)


LOG2E = 1.4426950408889634
N_PIECES = 3
CUM_BLOCK = 256


def _split3(x):
    hi = x.astype(BF16).astype(F32)
    r = x - hi
    mid = r.astype(BF16).astype(F32)
    lo = (r - mid).astype(BF16).astype(F32)
    return hi, mid, lo


def _forget_bias_kernel(lf_ref, tril_ref, pq_ref, pk_ref, oq_ref, ok_ref, exq_ref, exk_ref):
    tril = tril_ref[...]
    carry = jnp.zeros((1, LANES), F32)
    for blk in range(lf_ref.shape[0] // CUM_BLOCK):
        rows = slice(blk * CUM_BLOCK, (blk + 1) * CUM_BLOCK)
        c = carry
        for piece in _split3(lf_ref[rows, :]):
            c = c + jnp.dot(tril, piece.astype(BF16), preferred_element_type=F32)
        carry = c[CUM_BLOCK - 1:CUM_BLOCK, :]
        pieces = jnp.concatenate([p.astype(BF16) for p in _split3(c * LOG2E)], axis=1)
        for hp in range(N_HEADS // 2):
            exq_ref[hp, rows, :] = jnp.dot(pieces, pq_ref[hp], preferred_element_type=F32) + oq_ref[...]
            exk_ref[hp, rows, :] = jnp.dot(pieces, pk_ref[hp], preferred_element_type=F32) + ok_ref[...]


def _bias_placement():
    pq =np.zeros((N_HEADS // 2, N_PIECES * LANES, LANES), np.float32)
    pk = np.zeros_like(pq)
    oq = np.zeros((1, LANES), np.float32)
    ok = np.zeros((1, LANES), np.float32)
    for hp in range(N_HEADS // 2):
        for side in range(2):
            spare = HEAD_DIM * (1 - side)
            for piece in range(N_PIECES):
                src = piece * LANES + 2 * hp + side
                pq[hp, src, spare + piece] = 1.0
                pk[hp, src, spare + N_PIECES + piece] = -1.0
    for side in range(2):
        spare = HEAD_DIM * (1 - side)
        oq[0, spare + N_PIECES:spare + 2 * N_PIECES] = 1.0
        ok[0, spare:spare + N_PIECES] = 1.0
    return (jnp.asarray(pq, BF16), jnp.asarray(pk, BF16), jnp.asarray(oq), jnp.asarray(ok))


def _forget_bias(lfpad, batch, seq):
    pos = jnp.arange(CUM_BLOCK)
    tril = (pos[:, None] >= pos[None, :]).astype(BF16)
    pq, pk, oq, ok = _bias_placement()
    n_pairs = N_HEADS // 2
    const = lambda shape: pl.BlockSpec(shape, lambda b, n=len(shape): (0,) * n)
    out_spec = pl.BlockSpec((n_pairs, seq, LANES), lambda b: (0, b, 0))
    out_shape = jax.ShapeDtypeStruct((n_pairs, batch * seq, LANES), F32)
    return pl.pallas_call(
        _forget_bias_kernel,
        grid=(batch,),
        in_specs=[pl.BlockSpec((seq, LANES), lambda b: (b, 0)),
                  const(tril.shape), const(pq.shape), const(pk.shape), const(oq.shape), const(ok.shape)],
        out_specs=[out_spec, out_spec],
        out_shape=[out_shape, out_shape],
        compiler_params=_cparams(("arbitrary",)),
        name="forget_bias",
    )(lfpad, tril, pq, pk, oq, ok)


def _attn_kernel(qi_ref, kj_ref, q_ref, k_ref, v_ref, exq_ref, exk_ref, o_ref,
                 qa_sc, m_sc, l_sc, acc_sc, *, tq, tk):
    qi = qi_ref[pl.program_id(2)]
    kj = kj_ref[pl.program_id(2)]
    lane = lax.broadcasted_iota(jnp.int32, (1, LANES), 1)
    first = lane < HEAD_DIM

    @pl.when(kj == 0)
    def _():
        qs = q_ref[...] * (HEAD_DIM ** -0.5 * LOG2E)
        ex = exq_ref[0]
        qa_sc[0] = jnp.where(first, qs, ex).astype(BF16)
        qa_sc[1] = jnp.where(first, ex, qs).astype(BF16)
        m_sc[...] = jnp.full(m_sc.shape, NEG_BIG, F32)
        l_sc[...] = jnp.zeros(l_sc.shape, F32)
        acc_sc[...] = jnp.zeros(acc_sc.shape, F32)

    def block(masked):
        kx = k_ref[...]
        ex = exk_ref[0]
        vb = v_ref[...].astype(BF16)
        if masked:
            key_pos = kj * tk + lax.broadcasted_iota(jnp.int32, (tk, 1), 0)
            q_pos = qi * tq + lax.broadcasted_iota(jnp.int32, (1, tq), 1)
            causal = key_pos <= q_pos
        for h in range(2):
            ka = (jnp.where(first, kx, ex) if h == 0 else jnp.where(first, ex, kx)).astype(BF16)
            s = lax.dot_general(ka, qa_sc[h], (((1,), (1,)), ((), ())),
                                preferred_element_type=F32)
            if masked:
                s = jnp.where(causal, s, NEG_BIG)
            m_prev = m_sc[h]
            m_new = jnp.maximum(m_prev, jnp.max(s, axis=0, keepdims=True))
            alpha = jnp.exp2(m_prev - m_new)
            p = jnp.exp2(s - m_new)
            l_sc[h] = alpha * l_sc[h] + jnp.sum(p, axis=0, keepdims=True)
            pv = lax.dot_general(vb, p.astype(BF16), (((0,), (0,)), ((), ())),
                                 preferred_element_type=F32)
            acc_sc[h] = acc_sc[h] * alpha + pv
            m_sc[h] = m_new

    @pl.when(kj < qi)
    def _():
        block(False)

    @pl.when(kj == qi)
    def _():
        block(True)
        row = lax.broadcasted_iota(jnp.int32, (LANES, 1), 0)
        o_t = jnp.where(row < HEAD_DIM, acc_sc[0] / l_sc[0], acc_sc[1] / l_sc[1])
        o_ref[...] = o_t.T


def _prompt_attention(q, k, v, exq, exk, batch, seq):
    tq = tk = TQ
    nq = seq // tq
    pairs = [(i, j) for i in range(nq) for j in range(i + 1)]
    qi_tab = jnp.asarray([p[0] for p in pairs], jnp.int32)
    kj_tab = jnp.asarray([p[1] for p in pairs], jnp.int32)

    q_map = lambda b, hp, t, qi, kj: (b * nq + qi[t], hp)
    kv_map = lambda b, hp, t, qi, kj: (b * nq + kj[t], hp)
    grid_spec = pltpu.PrefetchScalarGridSpec(
        num_scalar_prefetch=2,
        grid=(batch, N_HEADS // 2, len(pairs)),
        in_specs=[pl.BlockSpec((tq, LANES), q_map),
                  pl.BlockSpec((tk, LANES), kv_map),
                  pl.BlockSpec((tk, LANES), kv_map),
                  pl.BlockSpec((1, tq, LANES), lambda b, hp, t, qi, kj: (hp, b * nq + qi[t], 0)),
                  pl.BlockSpec((1, tk, LANES), lambda b, hp, t, qi, kj: (hp, b * nq + kj[t], 0))],
        out_specs=pl.BlockSpec((tq, LANES), q_map),
        scratch_shapes=[pltpu.VMEM((2, tq, LANES), BF16),
                        pltpu.VMEM((2, 1, tq), F32), pltpu.VMEM((2, 1, tq), F32),
                        pltpu.VMEM((2, LANES, tq), F32)])
    return pl.pallas_call(
        functools.partial(_attn_kernel, tq=tq, tk=tk),
        grid_spec=grid_spec,
        out_shape=jax.ShapeDtypeStruct((batch * seq, WIDTH_A), F32),
        compiler_params=_cparams(("arbitrary",) * 3),
        name="prompt_attention",
    )(qi_tab, kj_tab, q, k, v, exq, exk)


def _zero_fill_padding(ends_ref, xs_ref, zero_sc, sem):
    def zero_tile(start):
        return pltpu.make_async_copy(zero_sc, xs_ref.at[pl.ds(pl.multiple_of(start, TE), TE), :], sem)

    zero_sc[...] = jnp.zeros(zero_sc.shape, F32)
    n_tiles = xs_ref.shape[0] // TE
    n_used = ends_ref[N_EXPERTS - 1] // TE
    for phase in ("start", "wait"):
        prev = 0
        for e in range(N_EXPERTS):
            end = ends_ref[e]

            @pl.when(end > prev)
            def _():
                getattr(zero_tile(end - TE), phase)()
            prev = end

        def unused(t, c):
            getattr(zero_tile(t * TE), phase)()
            return c

        lax.fori_loop(n_used, n_tiles, unused, 0)


def _decode_kernel(pt_ref, q_ref, kn_ref, vn_ref, cq_ref, tri_ref, *rest, n_pages):
    k_refs = rest[:n_pages]
    v_refs = rest[n_pages:2 * n_pages]
    lf_refs = rest[2 * n_pages:3 * n_pages]
    o_ref = rest[3 * n_pages]
    m_sc, l_sc, acc_sc, carry_sc = rest[3 * n_pages + 1:]
    j = pl.program_id(1)
    nj = pl.num_programs(1)
    head_of_lane = lax.broadcasted_iota(jnp.int32, (N_HEADS, WIDTH_A), 1) // HEAD_DIM
    own = head_of_lane == lax.broadcasted_iota(jnp.int32, (N_HEADS, WIDTH_A), 0)

    @pl.when(j == 0)
    def _():
        m_sc[...] = jnp.full(m_sc.shape, NEG_BIG, F32)
        l_sc[...] = jnp.zeros(l_sc.shape, F32)
        acc_sc[...] = jnp.zeros(acc_sc.shape, F32)
        carry_sc[...] = jnp.zeros(carry_sc.shape, F32)

    qbd_f = jnp.where(own, q_ref[0] * (HEAD_DIM ** -0.5), 0.0)
    qbd = qbd_f.astype(BF16)
    cq = cq_ref[0]

    lf = jnp.concatenate([r[0, 0] for r in lf_refs], axis=0)
    hi = lf.astype(BF16)
    r1 = lf - hi.astype(F32)
    mid = r1.astype(BF16)
    lo = (r1 - mid.astype(F32)).astype(BF16)
    tri = tri_ref[...]
    sums = (jnp.dot(hi, tri, preferred_element_type=F32)
            + jnp.dot(mid, tri, preferred_element_type=F32)
            + jnp.dot(lo, tri, preferred_element_type=F32))
    later = sums[:, :PAGE_SIZE]
    total = sums[:, PAGE_SIZE:]

    carry = carry_sc[...]
    bias = [None] * n_pages
    for i in reversed(range(n_pages)):
        rows = slice(i * N_HEADS, (i + 1) * N_HEADS)
        bias[i] = later[rows] + carry + cq
        carry = carry + total[rows]
    carry_sc[...] = carry

    def pair(refs, g):
        return jnp.concatenate([refs[2 * g][0, 0], refs[2 * g + 1][0, 0]], axis=1).astype(BF16)

    n_pairs = n_pages // 2
    scores = [jnp.dot(qbd, pair(k_refs, g), preferred_element_type=F32)
              + jnp.concatenate([bias[2 * g], bias[2 * g + 1]], axis=1) for g in range(n_pairs)]
    s_max = scores[0]
    for s in scores[1:]:
        s_max = jnp.maximum(s_max, s)
    m_prev = m_sc[...]
    m_new = jnp.maximum(m_prev, jnp.max(s_max, axis=-1, keepdims=True))
    alpha = jnp.exp(m_prev - m_new)
    acc = acc_sc[...] * alpha
    p_sum = jnp.zeros((N_HEADS, 2 * PAGE_SIZE), F32)
    for g in range(n_pairs):
        p = jnp.exp(scores[g] - m_new)
        p_sum = p_sum + p
        acc = acc + lax.dot_general(p.astype(BF16), pair(v_refs, g), (((1,), (1,)), ((), ())),
                                    preferred_element_type=F32)
    l_new = l_sc[...] * alpha + jnp.sum(p_sum, axis=-1, keepdims=True)
    m_sc[...] = m_new
    l_sc[...] = l_new
    acc_sc[...] = acc

    @pl.when(j == nj - 1)
    def _():
        s_new = jnp.sum(qbd_f * kn_ref[0], axis=-1, keepdims=True)
        m_f = jnp.maximum(m_new, s_new)
        a1 = jnp.exp(m_new - m_f)
        a2 = jnp.exp(s_new - m_f)
        o = (acc * a1 + a2 * vn_ref[0]) / (l_new * a1 + a2)
        o_ref[0] = jnp.sum(jnp.where(own, o, 0.0), axis=0, keepdims=True)


def _decode_attention(layer, q, k_new, v_new, cq, cache_kt, cache_vt, cache_lft, page_table, tri):
    nb, n_pages_total = page_table.shape
    npg = NP_STEP
    nj = n_pages_total // npg

    def page_map(i):
        def f(b, j, pt):
            return (layer, pt[b, (nj - 1 - j) * npg + i], 0, 0)
        return f

    row = lambda b, j, pt: (b, 0, 0)
    vec = pl.BlockSpec((1, 1, WIDTH_A), row)
    in_specs = [vec, vec, vec, pl.BlockSpec((1, N_HEADS, 1), row),
                pl.BlockSpec((PAGE_SIZE, 2 * PAGE_SIZE), lambda b, j, pt: (0, 0))]
    in_specs += [pl.BlockSpec((1, 1, WIDTH_A, PAGE_SIZE), page_map(i)) for i in range(npg)]
    in_specs += [pl.BlockSpec((1, 1, WIDTH_A, PAGE_SIZE), page_map(i)) for i in range(npg)]
    in_specs += [pl.BlockSpec((1, 1, N_HEADS, PAGE_SIZE), page_map(i)) for i in range(npg)]
    grid_spec = pltpu.PrefetchScalarGridSpec(
        num_scalar_prefetch=1,
        grid=(nb, nj),
        in_specs=in_specs,
        out_specs=vec,
        scratch_shapes=[pltpu.VMEM((N_HEADS, 1), F32),
                        pltpu.VMEM((N_HEADS, 1), F32),
                        pltpu.VMEM((N_HEADS, WIDTH_A), F32),
                        pltpu.VMEM((N_HEADS, PAGE_SIZE), F32)])
    return pl.pallas_call(
        functools.partial(_decode_kernel, n_pages=npg),
        grid_spec=grid_spec,
        out_shape=jax.ShapeDtypeStruct((nb, 1, WIDTH_A), F32),
        compiler_params=_cparams(("arbitrary", "arbitrary")),
        name="decode_attention",
    )(page_table, q, k_new, v_new, cq, tri,
      *([cache_kt] * npg), *([cache_vt] * npg), *([cache_lft] * npg))


def _pool_select(s2, s4, s8, s16):
    lane = lax.broadcasted_iota(jnp.int32, (1, WIDTH_C), 1)
    g = lane // GROUP_DIM_C
    return jnp.where(g == 0, s2, jnp.where(g == 1, s4, jnp.where(g == 2, s8, s16)))


def _merge_tail(a, b_out, d, gt_ref, x, mod_ref, wa_ref, wb_ref, wc_ref, wo_ref, pw_ref, ps_ref, gp_ref):
    c_out = _bdot(d, pw_ref[...]) * ps_ref[...]
    merged = (gt_ref[:, 0:D_MODEL] * _bdot(a, wa_ref[...])
              + gt_ref[:, D_MODEL:2 * D_MODEL] * _bdot(b_out, wb_ref[...])
              + gt_ref[:, 2 * D_MODEL:] * _bdot(c_out, wc_ref[...]))
    m = _bdot(merged, wo_ref[...])
    return x + mod_ref[:, 2, :] * _rms(m, gp_ref[...])


def _mix_prompt_kernel(a_ref, bb_ref, u_ref, pc_ref, uh_ref, ph_ref, gt_ref, x_ref, mod_ref,
                       cw_ref, wa_ref, wb_ref, wc_ref, wo_ref, pw_ref, ps_ref, gp_ref,
                       o_ref, ue_sc, pe_sc, *, tm, tiles_per_seq):
    i = pl.program_id(0)
    keep = jnp.where((i % tiles_per_seq) == 0, 0.0, 1.0)
    hu, hp = SUBLANES, 2 * SUBLANES
    ue_sc[0:hu, :] = uh_ref[...] * keep
    ue_sc[hu:, :] = u_ref[...]
    pe_sc[0:hp, :] = ph_ref[...] * keep
    pe_sc[hp:, :] = pc_ref[...]
    y_conv = (ue_sc[hu - 2:hu - 2 + tm, :] * cw_ref[0:1, :]
              + ue_sc[hu - 1:hu - 1 + tm, :] * cw_ref[1:2, :]
              + u_ref[...] * cw_ref[2:3, :])
    b_out = bb_ref[...] * y_conv
    p = pc_ref[...]

    def back(jj):
        return pe_sc[hp - jj:hp - jj + tm, :]

    s2 = p + back(1)
    s4 = s2 + back(2) + back(3)
    s8 = s4 + back(4) + back(5) + back(6) + back(7)
    s16 = s8
    for jj in range(8, 16):
        s16 = s16 + back(jj)
    pos = (i % tiles_per_seq) * tm + lax.broadcasted_iota(jnp.int32, (tm, 1), 0)
    posf = (pos + 1).astype(F32)
    means = _pool_select(s2 / jnp.minimum(2.0, posf), s4 / jnp.minimum(4.0, posf),
                         s8 / jnp.minimum(8.0, posf), s16 / jnp.minimum(16.0, posf))
    d = means - p
    o_ref[...] = _merge_tail(a_ref[...], b_out, d, gt_ref, x_ref[...], mod_ref,
                             wa_ref, wb_ref, wc_ref, wo_ref, pw_ref, ps_ref, gp_ref)


def _mix_sample_kernel(a_ref, bb_ref, u_ref, pc_ref, ch_ref, ph_ref, gt_ref, x_ref, mod_ref,
                       cw_ref, wa_ref, wb_ref, wc_ref, wo_ref, pw_ref, ps_ref, gp_ref, o_ref):
    u = u_ref[...]
    y_conv = ch_ref[:, 0, :] * cw_ref[0:1, :] + ch_ref[:, 1, :] * cw_ref[1:2, :] + u * cw_ref[2:3, :]
    b_out = bb_ref[...] * y_conv
    p = pc_ref[...]
    sums = []
    acc = p
    nxt = POOL_HIST - 1
    for w in POOL_WINDOWS:
        while POOL_HIST - nxt < w:
            acc = acc + ph_ref[:, nxt, :]
            nxt -= 1
        sums.append(acc / float(w))
    d = _pool_select(*sums) - p
    o_ref[...] = _merge_tail(a_ref[...], b_out, d, gt_ref, x_ref[...], mod_ref,
                             wa_ref, wb_ref, wc_ref, wo_ref, pw_ref, ps_ref, gp_ref)


def _const_specs(shapes):
    return [pl.BlockSpec(s, lambda i, n=len(s): (0,) * n) for s in shapes]


def _mix_prompt(a, bb, u, pc, gt, x, mod, wts, tm, seq):
    t = x.shape[0]
    tiles_per_seq = seq // tm
    hu, hp = SUBLANES, 2 * SUBLANES
    row = lambda w: pl.BlockSpec((tm, w), lambda i: (i, 0))
    in_specs = [row(WIDTH_A), row(WIDTH_B), row(WIDTH_B), row(WIDTH_C),
                pl.BlockSpec((hu, WIDTH_B), lambda i: (jnp.maximum(i * (tm // hu) - 1, 0), 0)),
                pl.BlockSpec((hp, WIDTH_C), lambda i: (jnp.maximum(i * (tm // hp) - 1, 0), 0)),
                row(N_BRANCHES * D_MODEL), row(D_MODEL), _mod_spec(tm, seq)]
    in_specs += _const_specs([w.shape for w in wts])
    return pl.pallas_call(
        functools.partial(_mix_prompt_kernel, tm=tm, tiles_per_seq=tiles_per_seq),
        grid=(t // tm,),
        in_specs=in_specs,
        out_specs=row(D_MODEL),
        out_shape=jax.ShapeDtypeStruct((t, D_MODEL), F32),
        scratch_shapes=[pltpu.VMEM((hu + tm, WIDTH_B), F32), pltpu.VMEM((hp + tm, WIDTH_C), F32)],
        compiler_params=_cparams(("arbitrary",)),
        name="mix_prompt",
    )(a, bb, u, pc, u, pc, gt, x, mod, *wts)


def _mix_sample(a, bb, u, pc, conv_hist, pool_hist, gt, x, mod, wts):
    t = x.shape[0]
    shapes = [(t, WIDTH_A), (t, WIDTH_B), (t, WIDTH_B), (t, WIDTH_C), conv_hist.shape, pool_hist.shape,
              (t, N_BRANCHES * D_MODEL), (t, D_MODEL), (t, N_MOD, D_MODEL)] + [w.shape for w in wts]
    return pl.pallas_call(
        _mix_sample_kernel,
        grid=(1,),
        in_specs=_const_specs(shapes),
        out_specs=pl.BlockSpec((t, D_MODEL), lambda i: (0, 0)),
        out_shape=jax.ShapeDtypeStruct((t, D_MODEL), F32),
        compiler_params=_cparams(("arbitrary",)),
        name="mix_sample",
    )(a, bb, u, pc, conv_hist, pool_hist, gt, x, mod, *wts)


def _first_index(mask, idx, big, axis):
    return jnp.min(jnp.where(mask, idx, big), axis=axis, keepdims=True)


def _router_kernel(x_ref, mod_ref, g_ref, wr_ref, rb_ref, upper_ref,
                   h_ref, eid_ref, wt_ref, rank_ref, comb_ref, cnt_ref, carry_sc, *, tm):
    i = pl.program_id(0)

    @pl.when(i == 0)
    def _():
        carry_sc[...] = jnp.zeros(carry_sc.shape, F32)

    h = _rms(x_ref[...], g_ref[...]) * (1.0 + mod_ref[:, 4, :]) + mod_ref[:, 3, :]
    h_ref[...] = h
    logits = lax.dot_general(wr_ref[...], h.astype(BF16), (((1,), (1,)), ((), ())),
                             preferred_element_type=F32)
    s = jax.nn.sigmoid(logits)
    sel = s + rb_ref[...]
    sg = sel.reshape(N_EXPERT_GROUPS, GROUP_SIZE, tm)
    in_idx = lax.broadcasted_iota(jnp.int32, sg.shape, 1)
    top1 = jnp.max(sg, axis=1, keepdims=True)
    f1 = _first_index(sg == top1, in_idx, GROUP_SIZE, 1)
    top2 = jnp.max(jnp.where(in_idx == f1, -jnp.inf, sg), axis=1, keepdims=True)
    gscore = top1 + top2
    g_idx = lax.broadcasted_iota(jnp.int32, gscore.shape, 0)
    gsel = jnp.zeros(gscore.shape, F32)
    for _ in range(TOPK_GROUPS):
        mx = jnp.max(gscore, axis=0, keepdims=True)
        hit = g_idx == _first_index(gscore == mx, g_idx, N_EXPERT_GROUPS, 0)
        gsel = jnp.where(hit, 1.0, gsel)
        gscore = jnp.where(hit, -jnp.inf, gscore)
    masked = jnp.where(gsel > 0.5, sg, -jnp.inf).reshape(N_EXPERTS, tm)
    e_idx = lax.broadcasted_iota(jnp.int32, masked.shape, 0)
    hits, ids, wts = [], [], []
    for _ in range(TOP_K):
        mx = jnp.max(masked, axis=0, keepdims=True)
        fe = _first_index(masked == mx, e_idx, N_EXPERTS, 0)
        hit = e_idx == fe
        hits.append(hit)
        ids.append(fe)
        wts.append(jnp.sum(jnp.where(hit, s, 0.0), axis=0, keepdims=True))
        masked = jnp.where(hit, -jnp.inf, masked)
    wsum = wts[0]
    for w in wts[1:]:
        wsum = wsum + w
    wts = [w / wsum * ROUTED_SCALE for w in wts]
    onehot = jnp.zeros((N_EXPERTS, tm), F32)
    comb = jnp.zeros((N_EXPERTS, tm), F32)
    for hit, w in zip(hits, wts):
        onehot = onehot + jnp.where(hit, 1.0, 0.0)
        comb = comb + jnp.where(hit, w, 0.0)
    before = jnp.dot(onehot.astype(BF16), upper_ref[...], preferred_element_type=F32) + carry_sc[...]
    ranks = [jnp.sum(jnp.where(hit, before, 0.0), axis=0, keepdims=True) for hit in hits]
    carry_new = carry_sc[...] + jnp.sum(onehot, axis=1, keepdims=True)
    carry_sc[...] = carry_new
    pad = jnp.zeros((SUBLANES - TOP_K, tm), F32)
    eid_ref[...] = jnp.concatenate(ids + [pad.astype(jnp.int32)], axis=0)
    wt_ref[...] = jnp.concatenate(wts + [pad], axis=0)
    rank_ref[...] = jnp.concatenate(ranks + [pad], axis=0).astype(jnp.int32)
    comb_ref[...] = comb
    cnt_ref[...] = carry_new.astype(jnp.int32)


def _router(x, mod, g, wr_t, rb, upper, tm, rows_per_mod):
    t = x.shape[0]
    col = lambda rows: pl.BlockSpec((rows, tm), lambda i: (0, i))
    return pl.pallas_call(
        functools.partial(_router_kernel, tm=tm),
        grid=(t // tm,),
        in_specs=[pl.BlockSpec((tm, D_MODEL), lambda i: (i, 0)),
                  _mod_spec(tm, rows_per_mod),
                  pl.BlockSpec((1, D_MODEL), lambda i: (0, 0)),
                  pl.BlockSpec((N_EXPERTS, D_MODEL), lambda i: (0, 0)),
                  pl.BlockSpec((N_EXPERTS, 1), lambda i: (0, 0)),
                  pl.BlockSpec((tm, tm), lambda i: (0, 0))],
        out_specs=[pl.BlockSpec((tm, D_MODEL), lambda i: (i, 0)),
                   col(SUBLANES), col(SUBLANES), col(SUBLANES), col(N_EXPERTS),
                   pl.BlockSpec((N_EXPERTS, 1), lambda i: (0, 0))],
        out_shape=[jax.ShapeDtypeStruct((t, D_MODEL), F32),
                   jax.ShapeDtypeStruct((SUBLANES, t), jnp.int32),
                   jax.ShapeDtypeStruct((SUBLANES, t), F32),
                   jax.ShapeDtypeStruct((SUBLANES, t), jnp.int32),
                   jax.ShapeDtypeStruct((N_EXPERTS, t), F32),
                   jax.ShapeDtypeStruct((N_EXPERTS, 1), jnp.int32)],
        scratch_shapes=[pltpu.VMEM((N_EXPERTS, 1), F32)],
        compiler_params=_cparams(("arbitrary",)),
        name="router",
    )(x, mod, g, wr_t, rb, upper)


def _row_copy_wait(src_tile, dst_tile, sem):
    pltpu.make_async_copy(src_tile, dst_tile, sem).wait()


def _dispatch_kernel(ends_ref, dest_ref, h_ref, o_ref, zero_sc, sem, *, tm):
    @pl.when(pl.program_id(0) == 0)
    def _():
        _zero_fill_padding(ends_ref, o_ref, zero_sc, sem)

    def rows(r0, c):
        for dr in range(DMA_UNROLL):
            r = r0 * DMA_UNROLL + dr
            for k in range(TOP_K):
                d = dest_ref[0, 0, r * TOP_K + k]
                pltpu.make_async_copy(h_ref.at[pl.ds(r, 1), :], o_ref.at[pl.ds(d, 1), :],
                                      sem).start(priority=(dr * TOP_K + k) % 2)
        return c

    lax.fori_loop(0, tm // DMA_UNROLL, rows, 0)
    for _ in range(TOP_K):
        _row_copy_wait(h_ref, o_ref.at[pl.ds(0, tm), :], sem)


def _dispatch(h, dest_tiles, ends, n_rows_pad, tm):
    t = h.shape[0]
    grid_spec = pltpu.PrefetchScalarGridSpec(
        num_scalar_prefetch=1,
        grid=(t // tm,),
        in_specs=[pl.BlockSpec((1, 1, tm * TOP_K), lambda i, ends: (i, 0, 0), memory_space=pltpu.SMEM),
                  pl.BlockSpec((tm, D_MODEL), lambda i, ends: (i, 0))],
        out_specs=pl.BlockSpec(memory_space=pl.ANY),
        scratch_shapes=[pltpu.VMEM((TE, D_MODEL), F32), pltpu.SemaphoreType.DMA(())])
    return pl.pallas_call(
        functools.partial(_dispatch_kernel, tm=tm),
        grid_spec=grid_spec,
        out_shape=jax.ShapeDtypeStruct((n_rows_pad, D_MODEL), F32),
        compiler_params=_cparams(("arbitrary",)),
        name="dispatch",
    )(ends, dest_tiles, h)


def _expert_kernel(te_ref, nv_ref, x_ref, wg_ref, wu_ref, wd_ref, y_ref, wgu_sc, wd_sc):
    t = pl.program_id(0)
    prev = te_ref[jnp.maximum(t - 1, 0)]
    fresh = jnp.logical_or(t == 0, te_ref[t] != prev)

    @pl.when(fresh)
    def _():
        wgu_sc[:, 0:D_EXPERT] = wg_ref[0, 0].astype(BF16)
        wgu_sc[:, D_EXPERT:] = wu_ref[0, 0].astype(BF16)
        wd_sc[...] = wd_ref[0, 0].astype(BF16)

    @pl.when(t < nv_ref[0])
    def _():
        gu = jnp.dot(x_ref[...].astype(BF16), wgu_sc[...], preferred_element_type=F32)
        a = _silu(gu[:, :D_EXPERT]) * gu[:, D_EXPERT:]
        y_ref[...] = jnp.dot(a.astype(BF16), wd_sc[...], preferred_element_type=F32)

    @pl.when(t >= nv_ref[0])
    def _():
        y_ref[...] = jnp.zeros(y_ref.shape, F32)


def _experts(layer, xs, tile_expert, n_valid, we_gate, we_up, we_down):
    n_tiles = xs.shape[0] // TE
    rows = lambda t, te, nv: (t, 0)
    wmap = lambda t, te, nv: (layer, te[t], 0, 0)
    grid_spec = pltpu.PrefetchScalarGridSpec(
        num_scalar_prefetch=2,
        grid=(n_tiles,),
        in_specs=[pl.BlockSpec((TE, D_MODEL), rows),
                  pl.BlockSpec((1, 1, D_MODEL, D_EXPERT), wmap),
                  pl.BlockSpec((1, 1, D_MODEL, D_EXPERT), wmap),
                  pl.BlockSpec((1, 1, D_EXPERT, D_MODEL), wmap)],
        out_specs=pl.BlockSpec((TE, D_MODEL), rows),
        scratch_shapes=[pltpu.VMEM((D_MODEL, 2 * D_EXPERT), BF16),
                        pltpu.VMEM((D_EXPERT, D_MODEL), BF16)])
    return pl.pallas_call(
        _expert_kernel,
        grid_spec=grid_spec,
        out_shape=jax.ShapeDtypeStruct((xs.shape[0], D_MODEL), F32),
        compiler_params=_cparams(("arbitrary",)),
        name="experts",
    )(tile_expert, n_valid, xs, we_gate, we_up, we_down)


def _dense_expert_kernel(h_ref, comb_ref, wg_ref, wu_ref, wd_ref, o_ref, acc_sc):
    e = pl.program_id(0)

    @pl.when(e == 0)
    def _():
        acc_sc[...] = jnp.zeros(acc_sc.shape, F32)

    hb = h_ref[...].astype(BF16)
    lane = lax.broadcasted_iota(jnp.int32, comb_ref.shape, 1)
    ce = jnp.sum(jnp.where(lane == e, comb_ref[...], 0.0), axis=1, keepdims=True)
    g = jnp.dot(hb, wg_ref[0, 0].astype(BF16), preferred_element_type=F32)
    u = jnp.dot(hb, wu_ref[0, 0].astype(BF16), preferred_element_type=F32)
    a = _silu(g) * u * ce
    acc_sc[...] += jnp.dot(a.astype(BF16), wd_ref[0, 0].astype(BF16), preferred_element_type=F32)

    @pl.when(e == pl.num_programs(0) - 1)
    def _():
        o_ref[...] = acc_sc[...]


def _dense_experts(layer, h, comb, we_gate, we_up, we_down):
    t = h.shape[0]
    wmap = lambda e: (layer, e, 0, 0)
    return pl.pallas_call(
        _dense_expert_kernel,
        grid=(N_EXPERTS,),
        in_specs=[pl.BlockSpec((t, D_MODEL), lambda e: (0, 0)),
                  pl.BlockSpec((t, N_EXPERTS), lambda e: (0, 0)),
                  pl.BlockSpec((1, 1, D_MODEL, D_EXPERT), wmap),
                  pl.BlockSpec((1, 1, D_MODEL, D_EXPERT), wmap),
                  pl.BlockSpec((1, 1, D_EXPERT, D_MODEL), wmap)],
        out_specs=pl.BlockSpec((t, D_MODEL), lambda e: (0, 0)),
        out_shape=jax.ShapeDtypeStruct((t, D_MODEL), F32),
        scratch_shapes=[pltpu.VMEM((t, D_MODEL), F32)],
        compiler_params=_cparams(("arbitrary",)),
        name="dense_experts",
    )(h, comb, we_gate, we_up, we_down)


def _shared_expert(h_ref, wgu_ref, wd_ref):
    gu = _bdot(h_ref[...], wgu_ref[...])
    a = _silu(gu[:, :D_EXPERT]) * gu[:, D_EXPERT:]
    return _bdot(a, wd_ref[...])


def _post_ffn(y, x_ref, mod_ref, g_ref):
    return x_ref[...] + mod_ref[:, 5, :] * _rms(y, g_ref[...])


def _ffn_tail(routed, x_ref, h_ref, mod_ref, wgu_ref, wd_ref, g_ref):
    return _post_ffn(routed + _shared_expert(h_ref, wgu_ref, wd_ref), x_ref, mod_ref, g_ref)


def _ffn_out_kernel(x_ref, r_ref, h_ref, mod_ref, wgu_ref, wd_ref, g_ref, o_ref):
    o_ref[...] = _ffn_tail(r_ref[...], x_ref, h_ref, mod_ref, wgu_ref, wd_ref, g_ref)


def _ffn_out(x, routed, h, mod, ws_gu, ws_d, g, tm, rows_per_mod):
    t = x.shape[0]
    row = pl.BlockSpec((tm, D_MODEL), lambda i: (i, 0))
    return pl.pallas_call(
        _ffn_out_kernel,
        grid=(t // tm,),
        in_specs=[row, row, row, _mod_spec(tm, rows_per_mod),
                  pl.BlockSpec((D_MODEL, 2 * D_EXPERT), lambda i: (0, 0)),
                  pl.BlockSpec((D_EXPERT, D_MODEL), lambda i: (0, 0)),
                  pl.BlockSpec((1, D_MODEL), lambda i: (0, 0))],
        out_specs=row,
        out_shape=jax.ShapeDtypeStruct((t, D_MODEL), F32),
        compiler_params=_cparams(("arbitrary",)),
        name="ffn_out",
    )(x, routed, h, mod, ws_gu, ws_d, g)


def _combine_kernel(dest_ref, y_ref, wt_ref, x_ref, h_ref, mod_ref, wgu_ref, wd_ref, g_ref,
                    o_ref, rows_sc, sem, *, tm):
    def rows(r0, c):
        for dr in range(DMA_UNROLL):
            r = r0 * DMA_UNROLL + dr
            for k in range(TOP_K):
                d = dest_ref[0, 0, r * TOP_K + k]
                pltpu.make_async_copy(y_ref.at[pl.ds(d, 1), :], rows_sc.at[k, pl.ds(r, 1), :],
                                      sem).start(priority=(dr * TOP_K + k) % 2)
        return c

    lax.fori_loop(0, tm // DMA_UNROLL, rows, 0)
    shared = _shared_expert(h_ref, wgu_ref, wd_ref)
    for k in range(TOP_K):
        _row_copy_wait(y_ref.at[pl.ds(0, tm), :], rows_sc.at[k], sem)
    routed = rows_sc[0] * wt_ref[:, 0:1]
    for k in range(1, TOP_K):
        routed = routed + rows_sc[k] * wt_ref[:, k:k + 1]
    o_ref[...] = _post_ffn(routed + shared, x_ref, mod_ref, g_ref)


def _combine_ffn_out(x, y, dest_tiles, wt, h, mod, ws_gu, ws_d, g, tm, rows_per_mod):
    t = x.shape[0]
    row = pl.BlockSpec((tm, D_MODEL), lambda i: (i, 0))
    return pl.pallas_call(
        functools.partial(_combine_kernel, tm=tm),
        grid=(t // tm,),
        in_specs=[pl.BlockSpec((1, 1, tm * TOP_K), lambda i: (i, 0, 0), memory_space=pltpu.SMEM),
                  pl.BlockSpec(memory_space=pl.ANY),
                  pl.BlockSpec((tm, SUBLANES), lambda i: (i, 0)),
                  row, row, _mod_spec(tm, rows_per_mod),
                  pl.BlockSpec((D_MODEL, 2 * D_EXPERT), lambda i: (0, 0)),
                  pl.BlockSpec((D_EXPERT, D_MODEL), lambda i: (0, 0)),
                  pl.BlockSpec((1, D_MODEL), lambda i: (0, 0))],
        out_specs=row,
        out_shape=jax.ShapeDtypeStruct((t, D_MODEL), F32),
        scratch_shapes=[pltpu.VMEM((TOP_K, tm, D_MODEL), F32), pltpu.SemaphoreType.DMA(())],
        compiler_params=_cparams(("arbitrary",)),
        name="combine_ffn_out",
    )(dest_tiles, y, wt, x, h, mod, ws_gu, ws_d, g)


def _pack_w_in(w):
    f = jnp.pad(w[:, OFF_F:OFF_BB], ((0, 0), (0, F_PAD - N_HEADS)))
    return jnp.concatenate([w[:, OFF_Q:OFF_F], w[:, OFF_BB:OFF_PC], w[:, OFF_PC:OFF_G], w[:, OFF_G:], f],
                           axis=1).astype(BF16)


def _block_diag(pool_w):
    out = jnp.zeros((WIDTH_C, WIDTH_C), F32)
    for g in range(len(POOL_WINDOWS)):
        lo = g * GROUP_DIM_C
        out = out.at[lo:lo + GROUP_DIM_C, lo:lo + GROUP_DIM_C].set(pool_w[g])
    return out.astype(BF16)


def kernel(x_prompt, x_sample, cache_k, cache_v, cache_logf, state_conv, state_pool, page_table,
           c_prompt, c_sample, w_in, b_forget, conv_w, pool_w, pool_scale, w_br_attn, w_br_conv,
           w_br_pool, w_out, g_pre_mix, g_post_mix, g_pre_ffn, g_post_ffn, w_ada, b_ada, w_router,
           router_bias, we_gate, we_up, we_down, ws_gate, ws_up, ws_down):
    bp, seq, d = x_prompt.shape
    bd = x_sample.shape[0]
    tp = bp * seq
    n_phys = cache_k.shape[1]

    xp = x_prompt.reshape(tp, d)
    xs = x_sample.reshape(bd, d)
    c_all = jnp.concatenate([c_prompt, c_sample], axis=0)
    cache_kt = cache_k.transpose(0, 1, 3, 4, 2).reshape(DEPTH, n_phys, WIDTH_A, PAGE_SIZE)
    cache_vt = cache_v.transpose(0, 1, 3, 4, 2).reshape(DEPTH, n_phys, WIDTH_A, PAGE_SIZE)
    cache_lft = cache_logf.transpose(0, 1, 3, 2)

    pos = jnp.arange(PAGE_SIZE)
    tri = jnp.concatenate([(pos[:, None] > pos[None, :]).astype(BF16),
                           jnp.ones((PAGE_SIZE, PAGE_SIZE), BF16)], axis=1)
    tok = jnp.arange(TM)
    upper = (tok[:, None] < tok[None, :]).astype(BF16)
    tok_s = jnp.arange(bd)
    upper_s = (tok_s[:, None] < tok_s[None, :]).astype(BF16)

    n_pairs = tp * TOP_K
    n_rows_pad = n_pairs + N_EXPERTS * TE
    n_tiles = n_rows_pad // TE
    experts_iota = jnp.arange(N_EXPERTS, dtype=jnp.int32)

    outs = {name: [] for name in ("kp", "vp", "lfp", "cvp", "plp", "ks", "vs", "lfs", "cvs", "pls")}
    for l in range(DEPTH):
        mod = _modulation(c_all, w_ada[l], b_ada[l]).reshape(bp + bd, N_MOD, d)
        mod_p, mod_s = mod[:bp], mod[bp:]
        w_pack = _pack_w_in(w_in[l])
        bf_pad = jnp.pad(b_forget[l], (0, F_PAD - N_HEADS)).reshape(1, F_PAD)
        mix_w = (conv_w[l], w_br_attn[l].astype(BF16), w_br_conv[l].astype(BF16),
                 w_br_pool[l].astype(BF16), w_out[l].astype(BF16), _block_diag(pool_w[l]),
                 pool_scale[l].reshape(1, -1), g_post_mix[l].reshape(1, -1))
        wr_t = w_router[l].T.astype(BF16)
        rb = router_bias[l].reshape(N_EXPERTS, 1)
        ws_gu = jnp.concatenate([ws_gate[l], ws_up[l]], axis=1).astype(BF16)
        ws_d = ws_down[l].astype(BF16)
        g1 = g_pre_mix[l].reshape(1, -1)
        g3 = g_pre_ffn[l].reshape(1, -1)
        g4 = g_post_ffn[l].reshape(1, -1)

        q, k, v, lfpad, bb, u, pc, gt = _in_proj(xp, mod_p, g1, w_pack, bf_pad, TM, seq)
        lf = lfpad[:, :N_HEADS].reshape(bp, seq, N_HEADS)
        exq, exk = _forget_bias(lfpad, bp, seq)
        a = _prompt_attention(q, k, v, exq, exk, bp, seq)
        xp = _mix_prompt(a, bb, u, pc, gt, xp, mod_p, mix_w, TM, seq)
        h2, eid, wt, rank, _, counts = _router(xp, mod_p, g3, wr_t, rb, upper, TM, seq)
        counts = counts[:, 0]
        padded = ((counts + TE - 1) // TE) * TE
        ends = jnp.cumsum(padded)
        starts = ends - padded
        eid6, rank6 = eid[:TOP_K], rank[:TOP_K]
        start_of = jnp.sum(jnp.where(eid6[:, :, None] == experts_iota, starts, 0), axis=-1)
        dest_tiles = (start_of + rank6).T.reshape(tp // TR, 1, TR * TOP_K)
        tile_start = jnp.arange(n_tiles, dtype=jnp.int32) * TE
        n_valid = ends[-1] // TE
        tile_expert = jnp.sum((tile_start[:, None] >= ends[None, :]).astype(jnp.int32), axis=1)
        last_expert = jnp.sum((ends[-1] - TE >= ends).astype(jnp.int32))
        tile_expert = jnp.minimum(tile_expert, last_expert)
        xs_sorted = _dispatch(h2, dest_tiles, ends, n_rows_pad, TR)
        y = _experts(l, xs_sorted, tile_expert, n_valid.reshape(1), we_gate, we_up, we_down)
        xp = _combine_ffn_out(xp, y, dest_tiles, wt.T, h2, mod_p, ws_gu, ws_d, g4, TR, seq)

        outs["kp"].append(k.reshape(bp, seq, N_HEADS, HEAD_DIM))
        outs["vp"].append(v.reshape(bp, seq, N_HEADS, HEAD_DIM))
        outs["lfp"].append(lf)
        outs["cvp"].append(u.reshape(bp, seq, WIDTH_B)[:, seq - (CONV_WIDTH - 1):])
        outs["plp"].append(pc.reshape(bp, seq, WIDTH_C)[:, seq - POOL_HIST:])

        q, k, v, lfpad, bb, u, pc, gt = _in_proj(xs, mod_s, g1, w_pack, bf_pad, bd, 1)
        lf = lfpad[:, :N_HEADS]
        vec = lambda z: z.reshape(bd, 1, WIDTH_A)
        a = _decode_attention(l, vec(q), vec(k), vec(v), lf.reshape(bd, N_HEADS, 1),
                              cache_kt, cache_vt, cache_lft, page_table, tri).reshape(bd, WIDTH_A)
        xs = _mix_sample(a, bb, u, pc, state_conv[l], state_pool[l], gt, xs, mod_s, mix_w)
        h2, _, _, _, comb, _ = _router(xs, mod_s, g3, wr_t, rb, upper_s, bd, 1)
        routed = _dense_experts(l, h2, comb.T, we_gate, we_up, we_down)
        xs = _ffn_out(xs, routed, h2, mod_s, ws_gu, ws_d, g4, bd, 1)

        outs["ks"].append(k.reshape(bd, 1, N_HEADS, HEAD_DIM))
        outs["vs"].append(v.reshape(bd, 1, N_HEADS, HEAD_DIM))
        outs["lfs"].append(lf.reshape(bd, 1, N_HEADS))
        outs["cvs"].append(jnp.concatenate([state_conv[l][:, 1:], u[:, None, :]], axis=1))
        outs["pls"].append(jnp.concatenate([state_pool[l][:, 1:], pc[:, None, :]], axis=1))

    st = {name: jnp.stack(vals) for name, vals in outs.items()}
    return (xp.reshape(bp, seq, d), xs.reshape(bd, 1, d), st["kp"], st["vp"], st["lfp"], st["cvp"],
            st["plp"], st["ks"], st["vs"], st["lfs"], st["cvs"], st["pls"])
```
